```python
import math
import jax, jax.numpy as jnp
from jax import lax
import numpy as np

D_MODEL = 1024
BATCH = 4
SEQ = 8192
DEPTH = 2

N_BRANCH = 4
BRANCH_W = D_MODEL // 4
GM_CHUNK = 128
GM_GROUPS = 4
ATT_HEAD_DIM = 64
ATT_HEADS = BRANCH_W // ATT_HEAD_DIM
IDX_HEADS = 4
IDX_DIM = 64
TOPK_MAX = 256
Q_BLOCK = 128
ROPE_THETA = 500000.0
ROPE_FRAC = 4
S5_GROUP = 16
S5_GROUPS = BRANCH_W // S5_GROUP
S5_STATE = 64
SSD_HEAD_DIM = 64
SSD_HEADS = BRANCH_W // SSD_HEAD_DIM
SSD_GROUPS = 2
SSD_STATE = 64
SSD_CONV = 4
SSD_CHUNK = 128
SSD_XBC = BRANCH_W + 2 * SSD_GROUPS * SSD_STATE
N_EXPERTS = 32
TOP_K = 8
N_EXPERT_GROUPS = 8
TOPK_GROUPS = 4
EXPERT_FF = D_MODEL // 4
SHARED_FF = EXPERT_FF
ROUTED_SCALE = 2.5
DN_ALPHA = (2 * DEPTH) ** 0.25
DN_BETA = (8 * DEPTH) ** -0.25
LN_EPS = 1e-5
RMS_EPS = 1e-6

IN_SPLITS = (BRANCH_W, BRANCH_W,
             ATT_HEADS * ATT_HEAD_DIM, ATT_HEAD_DIM, ATT_HEAD_DIM,
             IDX_HEADS * IDX_DIM, IDX_DIM, IDX_HEADS,
             BRANCH_W,
             BRANCH_W, SSD_XBC, SSD_HEADS)
D_IN = sum(IN_SPLITS)

kernel_name = "hybrid_gated_gmlp_dsa_s5_ssd_moe"


def layer_norm(x, g, b):
    xf = x.astype(jnp.float32)
    mu = jnp.mean(xf, -1, keepdims=True)
    var = jnp.mean(jnp.square(xf - mu), -1, keepdims=True)
    return ((xf - mu) * lax.rsqrt(var + LN_EPS) * g.astype(jnp.float32) + b.astype(jnp.float32)).astype(x.dtype)


def rope_tables(positions):
    rot = ATT_HEAD_DIM // ROPE_FRAC
    inv = ROPE_THETA ** (-jnp.arange(0, rot, 2, dtype=jnp.float32) / rot)
    ang = positions.astype(jnp.float32)[..., None] * inv
    return jnp.cos(ang), jnp.sin(ang)


def apply_partial_rope(t, cos, sin):
    half = cos.shape[-1]
    shp = cos.shape[:2] + (1,) * (t.ndim - 3) + (half,)
    cs, sn = cos.reshape(shp), sin.reshape(shp)
    tf = t.astype(jnp.float32)
    t1, t2 = tf[..., :half], tf[..., half:2 * half]
    return jnp.concatenate([t1 * cs - t2 * sn, t2 * cs + t1 * sn, tf[..., 2 * half:]], -1).astype(t.dtype)


def gmlp_mixer(u, v, ln_g, ln_b, sg_w, sg_b):
    B_, S_, W = u.shape
    u = jax.nn.gelu(u)
    v = layer_norm(jax.nn.gelu(v), ln_g, ln_b)
    v = v.reshape(B_, S_ // GM_CHUNK, GM_CHUNK, GM_GROUPS, W // GM_GROUPS)
    w = sg_w * jnp.tril(jnp.ones((GM_CHUNK, GM_CHUNK), sg_w.dtype))
    sv = jnp.einsum('gts,bnsgc->bntgc', w, v) + sg_b.T[:, :, None]
    return u * sv.reshape(B_, S_, W)


def dsa_attention(q, k, v, qi, ki, wi):
    B_, S_ = q.shape[:2]
    n_sel = min(TOPK_MAX, S_ // 4)
    nb = S_ // Q_BLOCK
    wi = wi.astype(jnp.float32) * (IDX_HEADS ** -0.5 * IDX_DIM ** -0.5)
    key_pos = jnp.arange(S_)

    def blocks(t):
        return jnp.moveaxis(t.reshape((B_, nb, Q_BLOCK) + t.shape[2:]), 1, 0)

    def one_block(args):
        qb, qib, wib, start = args
        qpos = start + jnp.arange(Q_BLOCK)
        idx_logits = jnp.einsum('bqhd,bsd->bqhs', qib, ki).astype(jnp.float32)
        score = jnp.einsum('bqhs,bqh->bqs', jax.nn.relu(idx_logits), wib)
        causal = key_pos[None, :] <= qpos[:, None]
        score = jnp.where(causal[None], score, -jnp.inf)
        _, sel = lax.top_k(score, n_sel)
        k_sel = jax.vmap(lambda kb, ib: kb[ib])(k, sel)
        v_sel = jax.vmap(lambda vb, ib: vb[ib])(v, sel)
        valid = sel <= qpos[None, :, None]
        logits = jnp.einsum('bqhd,bqnd->bhqn', qb, k_sel).astype(jnp.float32) * ATT_HEAD_DIM ** -0.5
        logits = jnp.where(valid[:, None], logits, -jnp.inf)
        p = jax.nn.softmax(logits, axis=-1).astype(v.dtype)
        return jnp.einsum('bhqn,bqnd->bqhd', p, v_sel)

    starts = jnp.arange(nb) * Q_BLOCK
    out = lax.map(one_block, (blocks(q), blocks(qi), blocks(wi), starts))
    return jnp.moveaxis(out, 0, 1).reshape(B_, S_, ATT_HEADS * ATT_HEAD_DIM)


def s5_mixer(u, lam_re, lam_im, log_step, b_re, b_im, c_re, c_im, d_skip, glu_w, glu_b):
    B_, S_, W = u.shape
    f32 = jnp.float32
    uf = u.astype(f32)
    ug = uf.reshape(B_, S_, S5_GROUPS, S5_GROUP)
    lr, li = lam_re.astype(f32), lam_im.astype(f32)
    step = jnp.exp(log_step.astype(f32))[:, None]
    mag = jnp.exp(lr * step)
    ab_re, ab_im = mag * jnp.cos(li * step), mag * jnp.sin(li * step)
    den = lr * lr + li * li
    nr = ab_re - 1.0
    cr = (nr * lr + ab_im * li) / den
    ci = (ab_im * lr - nr * li) / den
    br, bi = b_re.astype(f32), b_im.astype(f32)
    bb_re = cr[..., None] * br - ci[..., None] * bi
    bb_im = cr[..., None] * bi + ci[..., None] * br
    x_re = jnp.einsum('bsgc,gpc->bsgp', ug, bb_re)
    x_im = jnp.einsum('bsgc,gpc->bsgp', ug, bb_im)
    a_re = jnp.broadcast_to(ab_re, x_re.shape)
    a_im = jnp.broadcast_to(ab_im, x_im.shape)

    def combine(e1, e2):
        a1r, a1i, b1r, b1i = e1
        a2r, a2i, b2r, b2i = e2
        return (a2r * a1r - a2i * a1i, a2r * a1i + a2i * a1r,
                a2r * b1r - a2i * b1i + b2r, a2r * b1i + a2i * b1r + b2i)

    _, _, h_re, h_im = lax.associative_scan(combine, (a_re, a_im, x_re, x_im), axis=1)
    y = (jnp.einsum('bsgp,gcp->bsgc', h_re, c_re.astype(f32))
         - jnp.einsum('bsgp,gcp->bsgc', h_im, c_im.astype(f32)))
    y = y.reshape(B_, S_, W) + d_skip.astype(f32) * uf
    y = jax.nn.gelu(y).astype(u.dtype)
    return y * jax.nn.sigmoid(y @ glu_w + glu_b)


def segsum(a):
    T = a.shape[-1]
    ar = jnp.broadcast_to(a[..., None], a.shape + (T,))
    strict = jnp.tril(jnp.ones((T, T), bool), -1)
    cs = jnp.cumsum(jnp.where(strict, ar, 0.0), axis=-2)
    return jnp.where(jnp.tril(jnp.ones((T, T), bool)), cs, -jnp.inf)


def ssd_mixer(z, xbc, dt, conv_w, conv_b, dt_bias, a_log, d_skip, norm_g):
    B_, S_, _ = xbc.shape
    f32 = jnp.float32
    conv = lax.conv_general_dilated(xbc, conv_w[:, None, :], window_strides=(1,),
                                    padding=[(SSD_CONV - 1, 0)],
                                    dimension_numbers=('NWC', 'WIO', 'NWC'),
                                    feature_group_count=xbc.shape[-1])
    xbc = jax.nn.silu(conv + conv_b)
    x, bm, cm = jnp.split(xbc, [BRANCH_W, BRANCH_W + SSD_GROUPS * SSD_STATE], axis=-1)
    hpg = SSD_HEADS // SSD_GROUPS
    nc = S_ // SSD_CHUNK
    dt = jax.nn.softplus(dt.astype(f32) + dt_bias.astype(f32))
    a = -jnp.exp(a_log.astype(f32))
    xf = x.astype(f32)
    X = xf.reshape(B_, nc, SSD_CHUNK, SSD_GROUPS, hpg, SSD_HEAD_DIM)
    Xdt = X * dt.reshape(B_, nc, SSD_CHUNK, SSD_GROUPS, hpg)[..., None]
    Adt = (dt * a).reshape(B_, nc, SSD_CHUNK, SSD_GROUPS, hpg).transpose(0, 3, 4, 1, 2)
    Bm = bm.astype(f32).reshape(B_, nc, SSD_CHUNK, SSD_GROUPS, SSD_STATE)
    Cm = cm.astype(f32).reshape(B_, nc, SSD_CHUNK, SSD_GROUPS, SSD_STATE)
    a_cum = jnp.cumsum(Adt, axis=-1)
    Lmat = jnp.exp(segsum(Adt))
    y_diag = jnp.einsum('bclgn,bcsgn,bgjcls,bcsgjp->bclgjp', Cm, Bm, Lmat, Xdt)
    decay_states = jnp.exp(a_cum[..., -1:] - a_cum)
    states = jnp.einsum('bclgn,bgjcl,bclgjp->bcgjpn', Bm, decay_states, Xdt)
    states = jnp.concatenate([jnp.zeros_like(states[:, :1]), states], axis=1)
    chunk_a = jnp.pad(a_cum[..., -1], ((0, 0), (0, 0), (0, 0), (1, 0)))
    decay_chunk = jnp.exp(segsum(chunk_a))
    states = jnp.einsum('bgjzc,bcgjpn->bzgjpn', decay_chunk, states)[:, :-1]
    y_off = jnp.einsum('bclgn,bcgjpn,bgjcl->bclgjp', Cm, states, jnp.exp(a_cum))
    y = (y_diag + y_off).reshape(B_, S_, SSD_HEADS, SSD_HEAD_DIM)
    y = y + d_skip.astype(f32)[:, None] * xf.reshape(B_, S_, SSD_HEADS, SSD_HEAD_DIM)
    y = (y.reshape(B_, S_, BRANCH_W) * jax.nn.silu(z.astype(f32))).reshape(B_, S_, SSD_GROUPS, -1)
    y = y * lax.rsqrt(jnp.mean(jnp.square(y), -1, keepdims=True) + RMS_EPS)
    return (y.reshape(B_, S_, BRANCH_W) * norm_g.astype(f32)).astype(z.dtype)


def hybrid_mixer(h, cos, sin, w_in, gm_ln_g, gm_ln_b, gm_w, gm_b,
                 s5_lam_re, s5_lam_im, s5_log_step, s5_b_re, s5_b_im, s5_c_re, s5_c_im, s5_d,
                 s5_glu_w, s5_glu_b, ssd_conv_w, ssd_conv_b, ssd_dt_bias, ssd_a_log, ssd_d, ssd_norm_g,
                 w_branch, w_gate, w_out):
    B_, S_, _ = h.shape
    proj = h @ w_in
    cuts = [int(i) for i in np.cumsum(IN_SPLITS)[:-1]]
    gm_u, gm_v, q, k, v, qi, ki, wi, s5_u, z, xbc, dt = jnp.split(proj, cuts, axis=-1)
    y_a = gmlp_mixer(gm_u, gm_v, gm_ln_g, gm_ln_b, gm_w, gm_b)
    q = apply_partial_rope(q.reshape(B_, S_, ATT_HEADS, ATT_HEAD_DIM), cos, sin)
    k = apply_partial_rope(k, cos, sin)
    qi = apply_partial_rope(qi.reshape(B_, S_, IDX_HEADS, IDX_DIM), cos, sin)
    ki = apply_partial_rope(ki, cos, sin)
    y_b = dsa_attention(q, k, v, qi, ki, wi)
    y_c = s5_mixer(s5_u, s5_lam_re, s5_lam_im, s5_log_step, s5_b_re, s5_b_im, s5_c_re, s5_c_im,
                   s5_d, s5_glu_w, s5_glu_b)
    y_d = ssd_mixer(z, xbc, dt, ssd_conv_w, ssd_conv_b, ssd_dt_bias, ssd_a_log, ssd_d, ssd_norm_g)
    branches = (y_a, y_b, y_c, y_d)
    merged = jax.nn.sigmoid(h @ w_gate[0]) * (branches[0].astype(h.dtype) @ w_branch[0])
    for i in range(1, N_BRANCH):
        merged = merged + jax.nn.sigmoid(h @ w_gate[i]) * (branches[i].astype(h.dtype) @ w_branch[i])
    return merged @ w_out


def moe_ffn(h, router_w, router_bias, w1, w3, w2, sw1, sw3, sw2):
    B_, S_, _ = h.shape
    f32 = jnp.float32
    scores = jax.nn.sigmoid((h @ router_w).astype(f32))
    biased = scores + router_bias.astype(f32)
    grp = biased.reshape(B_, S_, N_EXPERT_GROUPS, N_EXPERTS // N_EXPERT_GROUPS)
    grp_score = jnp.sum(lax.top_k(grp, 2)[0], -1)
    _, gidx = lax.top_k(grp_score, TOPK_GROUPS)
    gmask = jnp.sum(jax.nn.one_hot(gidx, N_EXPERT_GROUPS, dtype=f32), -2) > 0
    emask = jnp.repeat(gmask, N_EXPERTS // N_EXPERT_GROUPS, axis=-1)
    _, eidx = lax.top_k(jnp.where(emask, biased, -jnp.inf), TOP_K)
    w = jnp.take_along_axis(scores, eidx, axis=-1)
    w = w / jnp.sum(w, -1, keepdims=True) * ROUTED_SCALE
    gates = jnp.sum(jax.nn.one_hot(eidx, N_EXPERTS, dtype=f32) * w[..., None], -2)

    def per_seq(args):
        hs, gs = args
        a = jnp.einsum('sd,edf->sef', hs, w1)
        b = jnp.einsum('sd,edf->sef', hs, w3)
        act = jax.nn.silu(a) * b * gs[..., None].astype(hs.dtype)
        return jnp.einsum('sef,efd->sd', act, w2)

    routed = lax.map(per_seq, (h, gates))
    shared = (jax.nn.silu(h @ sw1) * (h @ sw3)) @ sw2
    return routed + shared


def adaln(c, w, b):
    m = c @ w + b
    shift, scale, gate = jnp.split(m, 3, axis=-1)
    return shift[:, None], scale[:, None], gate[:, None]


def setup_inputs(seed: int = 0) -> dict:
    key = jax.random.key(seed)
    keys = jax.random.split(key, 64)
    cnt = [0]
    f32 = jnp.float32
    L, D, W = DEPTH, D_MODEL, BRANCH_W

    def nk():
        k = keys[cnt[0]]
        cnt[0] += 1
        return k

    def nrm(shape, scale):
        return jax.random.normal(nk(), shape, f32) * scale

    def near_one(shape, s=0.02):
        return 1.0 + nrm(shape, s)

    x = nrm((BATCH, SEQ, D), 1.0)
    c = nrm((BATCH, D), 1.0)
    offset = jax.random.randint(nk(), (BATCH, 1), 0, 4096)
    positions = (offset + jnp.arange(SEQ)[None, :]).astype(jnp.int32)

    mod1_w = nrm((L, D, 3 * D), 0.2 * D ** -0.5)
    mod1_b = nrm((L, 3 * D), 0.02)
    w_in = nrm((L, D, D_IN), D ** -0.5)
    gm_ln_g = near_one((L, W))
    gm_ln_b = nrm((L, W), 0.02)
    gm_w = nrm((L, GM_GROUPS, GM_CHUNK, GM_CHUNK), GM_CHUNK ** -0.5)
    gm_b = near_one((L, GM_GROUPS, GM_CHUNK), 0.1)

    s5_lam_re = -0.5 + nrm((L, S5_GROUPS, S5_STATE), 0.01)
    s5_lam_im = math.pi * jnp.arange(S5_STATE, dtype=f32) + nrm((L, S5_GROUPS, S5_STATE), 0.01)
    s5_log_step = jax.random.uniform(nk(), (L, S5_GROUPS), f32, math.log(1e-3), math.log(1e-1))
    s5_b_re = nrm((L, S5_GROUPS, S5_STATE, S5_GROUP), (2 * S5_GROUP) ** -0.5)
    s5_b_im = nrm((L, S5_GROUPS, S5_STATE, S5_GROUP), (2 * S5_GROUP) ** -0.5)
    s5_c_re = nrm((L, S5_GROUPS, S5_GROUP, S5_STATE), S5_STATE ** -0.5)
    s5_c_im = nrm((L, S5_GROUPS, S5_GROUP, S5_STATE), S5_STATE ** -0.5)
    s5_d = nrm((L, W), 1.0)
    s5_glu_w = nrm((L, W, W), W ** -0.5)
    s5_glu_b = nrm((L, W), 0.02)

    ssd_conv_w = nrm((L, SSD_CONV, SSD_XBC), SSD_CONV ** -0.5)
    ssd_conv_b = nrm((L, SSD_XBC), 0.02)
    dt0 = jnp.exp(jax.random.uniform(nk(), (L, SSD_HEADS), f32, math.log(1e-3), math.log(1e-1)))
    ssd_dt_bias = dt0 + jnp.log(-jnp.expm1(-dt0))
    ssd_a_log = jnp.log(jax.random.uniform(nk(), (L, SSD_HEADS), f32, 1.0, 16.0))
    ssd_d = near_one((L, SSD_HEADS), 0.1)
    ssd_norm_g = near_one((L, W))

    w_branch = nrm((L, N_BRANCH, W, D), W ** -0.5)
    w_gate = nrm((L, N_BRANCH, D, D), D ** -0.5)
    w_out = nrm((L, D, D), D ** -0.5 * DN_BETA)
    ln1_g = near_one((L, D))
    ln1_b = nrm((L, D), 0.02)

    mod2_w = nrm((L, D, 3 * D), 0.2 * D ** -0.5)
    mod2_b = nrm((L, 3 * D), 0.02)
    router_w = nrm((L, D, N_EXPERTS), D ** -0.5)
    router_bias = nrm((L, N_EXPERTS), 0.01)
    exp_w1 = nrm((L, N_EXPERTS, D, EXPERT_FF), D ** -0.5)
    exp_w3 = nrm((L, N_EXPERTS, D, EXPERT_FF), D ** -0.5)
    exp_w2 = nrm((L, N_EXPERTS, EXPERT_FF, D), EXPERT_FF ** -0.5 * DN_BETA)
    sh_w1 = nrm((L, D, SHARED_FF), D ** -0.5)
    sh_w3 = nrm((L, D, SHARED_FF), D ** -0.5)
    sh_w2 = nrm((L, SHARED_FF, D), SHARED_FF ** -0.5 * DN_BETA)
    ln2_g = near_one((L, D))
    ln2_b = nrm((L, D), 0.02)

    return {"x": x, "c": c, "positions": positions,
            "mod1_w": mod1_w, "mod1_b": mod1_b, "w_in": w_in,
            "gm_ln_g": gm_ln_g, "gm_ln_b": gm_ln_b, "gm_w": gm_w, "gm_b": gm_b,
            "s5_lam_re": s5_lam_re, "s5_lam_im": s5_lam_im, "s5_log_step": s5_log_step,
            "s5_b_re": s5_b_re, "s5_b_im": s5_b_im, "s5_c_re": s5_c_re, "s5_c_im": s5_c_im,
            "s5_d": s5_d, "s5_glu_w": s5_glu_w, "s5_glu_b": s5_glu_b,
            "ssd_conv_w": ssd_conv_w, "ssd_conv_b": ssd_conv_b, "ssd_dt_bias": ssd_dt_bias,
            "ssd_a_log": ssd_a_log, "ssd_d": ssd_d, "ssd_norm_g": ssd_norm_g,
            "w_branch": w_branch, "w_gate": w_gate, "w_out": w_out, "ln1_g": ln1_g, "ln1_b": ln1_b,
            "mod2_w": mod2_w, "mod2_b": mod2_b, "router_w": router_w, "router_bias": router_bias,
            "exp_w1": exp_w1, "exp_w3": exp_w3, "exp_w2": exp_w2,
            "sh_w1": sh_w1, "sh_w3": sh_w3, "sh_w2": sh_w2, "ln2_g": ln2_g, "ln2_b": ln2_b}


def reference(x, c, positions, mod1_w, mod1_b, w_in, gm_ln_g, gm_ln_b, gm_w, gm_b,
              s5_lam_re, s5_lam_im, s5_log_step, s5_b_re, s5_b_im, s5_c_re, s5_c_im,
              s5_d, s5_glu_w, s5_glu_b, ssd_conv_w, ssd_conv_b, ssd_dt_bias, ssd_a_log, ssd_d, ssd_norm_g,
              w_branch, w_gate, w_out, ln1_g, ln1_b, mod2_w, mod2_b, router_w, router_bias,
              exp_w1, exp_w3, exp_w2, sh_w1, sh_w3, sh_w2, ln2_g, ln2_b):
    cos, sin = rope_tables(positions)
    for l in range(DEPTH):
        shift, scale, gate = adaln(c, mod1_w[l], mod1_b[l])
        h = x * (1.0 + scale) + shift
        mix = hybrid_mixer(h, cos, sin, w_in[l], gm_ln_g[l], gm_ln_b[l], gm_w[l], gm_b[l],
                           s5_lam_re[l], s5_lam_im[l], s5_log_step[l], s5_b_re[l], s5_b_im[l],
                           s5_c_re[l], s5_c_im[l], s5_d[l], s5_glu_w[l], s5_glu_b[l],
                           ssd_conv_w[l], ssd_conv_b[l], ssd_dt_bias[l], ssd_a_log[l], ssd_d[l],
                           ssd_norm_g[l], w_branch[l], w_gate[l], w_out[l])
        x = layer_norm(DN_ALPHA * x + (1.0 + gate) * mix, ln1_g[l], ln1_b[l])

        shift, scale, gate = adaln(c, mod2_w[l], mod2_b[l])
        h = x * (1.0 + scale) + shift
        ffn = moe_ffn(h, router_w[l], router_bias[l], exp_w1[l], exp_w3[l], exp_w2[l],
                      sh_w1[l], sh_w3[l], sh_w2[l])
        x = layer_norm(DN_ALPHA * x + (1.0 + gate) * ffn, ln2_g[l], ln2_b[l])
    return x
```

```python
import functools
import math

import jax
import jax.numpy as jnp
import numpy as np
from jax import lax
from jax.experimental import pallas as pl
from jax.experimental.pallas import tpu as pltpu

F32 = jnp.float32
BF16 = jnp.bfloat16

D_MODEL = 1024
DEPTH = 2
BRANCH_W = 256
GM_CHUNK = 128
GM_GROUPS = 4
ATT_HEAD_DIM = 64
ATT_HEADS = 4
IDX_HEADS = 4
IDX_DIM = 64
TOPK_MAX = 256
ROPE_THETA = 500000.0
ROPE_HALF = 8
S5_GROUP = 16
S5_GROUPS = 16
S5_STATE = 64
S5_CHUNK = 32
SSD_HEAD_DIM = 64
SSD_HEADS = 4
SSD_GROUPS = 2
SSD_STATE = 64
SSD_CONV = 4
SSD_CHUNK = 128
SSD_XBC = 512
N_EXPERTS = 32
TOP_K = 8
N_EXPERT_GROUPS = 8
TOPK_GROUPS = 4
EXPERT_FF = 256
ROUTED_SCALE = 2.5
DN_ALPHA = (2 * DEPTH) ** 0.25
LN_EPS = 1e-5
RMS_EPS = 1e-6

LANES = 128
SUBLANES = 8
VMEM_LIMIT_BYTES = 56 * 1024 * 1024

C_GMU, C_GMV, C_Q, C_QI, C_KV, C_S5, C_Z, C_XBC = 0, 256, 512, 768, 1024, 1280, 1536, 1792
D_IN_PACKED = 2304
KV_K, KV_V, KV_KI, KV_WI, KV_DT = 0, 64, 128, 192, 196

NEG_BIG = -1e30
INT_MIN = -2 ** 31


def _cparams(*sem):
    return pltpu.CompilerParams(dimension_semantics=sem, vmem_limit_bytes=VMEM_LIMIT_BYTES)


def _const_spec(shape):
    nd = len(shape)
    return pl.BlockSpec(shape, lambda *_: (0,) * nd, pipeline_mode=pl.Buffered(1))


def _layer_norm(v, g, b):
    mu = jnp.mean(v, axis=-1, keepdims=True)
    d = v - mu
    var = jnp.mean(d * d, axis=-1, keepdims=True)
    return d * lax.rsqrt(var + LN_EPS) * g + b


def _sigmoid(v):
    return 1.0 / (1.0 + jnp.exp(-v))


def _silu(v):
    return v * _sigmoid(v)


def _adaln_kernel(c_ref, w_ref, b_ref, o_ref):
    o_ref[...] = jnp.dot(c_ref[...], w_ref[...], preferred_element_type=F32,
                         precision=lax.Precision.HIGHEST) + b_ref[...]


def adaln_all(c_pad, w_all, b_all):
    m, d, d3 = w_all.shape
    bn = 512
    return pl.pallas_call(
        _adaln_kernel,
        grid=(m, d3 // bn),
        in_specs=[pl.BlockSpec((SUBLANES, d), lambda i, j: (0, 0)),
                  pl.BlockSpec((None, d, bn), lambda i, j: (i, 0, j)),
                  pl.BlockSpec((None, 1, bn), lambda i, j: (i, 0, j))],
        out_specs=pl.BlockSpec((None, SUBLANES, bn), lambda i, j: (i, 0, j)),
        out_shape=jax.ShapeDtypeStruct((m, SUBLANES, d3), F32),
        compiler_params=_cparams("parallel", "parallel"),
        name="adaln",
    )(c_pad, w_all, b_all)


def _rope128(t, c, a, b):
    return t * c + pltpu.roll(t, LANES - ROPE_HALF, axis=1) * a + pltpu.roll(t, ROPE_HALF, axis=1) * b


def _inproj_kernel(x_ref, shift_ref, scale_ref, w_ref, rc_ref, ra_ref, rb_ref,
                   lng_ref, lnb_ref, gmw_ref, gmb_ref,
                   ya_ref, q_ref, qi_ref, kvk_ref, aux_ref, s5u_ref, z_ref, xbc_ref):
    t = x_ref.shape[0]
    h = (x_ref[...] * (1.0 + scale_ref[...]) + shift_ref[...]).astype(BF16)
    rc, ra, rb = rc_ref[...], ra_ref[...], rb_ref[...]

    def proj(c0, width):
        return jnp.dot(h, w_ref[:, c0:c0 + width], preferred_element_type=F32)

    for c0, o_ref, mul in ((C_Q, q_ref, ATT_HEAD_DIM ** -0.5), (C_QI, qi_ref, 1.0)):
        p = proj(c0, 256)
        o_ref[:, 0:128] = (_rope128(p[:, 0:128], rc, ra, rb) * mul).astype(BF16)
        o_ref[:, 128:256] = (_rope128(p[:, 128:256], rc, ra, rb) * mul).astype(BF16)

    p = proj(C_KV, 256)
    lane = lax.broadcasted_iota(jnp.int32, (t, LANES), 1)
    first = lane < ATT_HEAD_DIM
    rc1 = jnp.where(first, rc, 1.0)
    ra1 = jnp.where(first, ra, 0.0)
    rb1 = jnp.where(first, rb, 0.0)
    kv = _rope128(p[:, 0:128], rc1, ra1, rb1)
    kiw = _rope128(p[:, 128:256], rc1, ra1, rb1)
    kvk_ref[:, 0:128] = kv.astype(BF16)
    kvk_ref[:, 128:256] = kiw.astype(BF16)
    aux_ref[...] = kiw

    s5u_ref[...] = proj(C_S5, 256)
    z_ref[...] = proj(C_Z, 256)
    xbc_ref[...] = proj(C_XBC, 512)

    u = jax.nn.gelu(proj(C_GMU, 256))
    v = _layer_norm(jax.nn.gelu(proj(C_GMV, 256)), lng_ref[...], lnb_ref[...]).astype(BF16)
    gw = GM_CHUNK
    cw = BRANCH_W // GM_GROUPS
    for ci in range(t // gw):
        rows = slice(ci * gw, (ci + 1) * gw)
        for g in range(GM_GROUPS):
            cols = slice(g * cw, (g + 1) * cw)
            sv = jnp.dot(gmw_ref[g], v[rows, cols], preferred_element_type=F32) + gmb_ref[:, cols]
            ya_ref[rows, cols] = (u[rows, cols] * sv).astype(BF16)


def inproj(x2, mod, w_in_p, rc, ra, rb, lng, lnb, gmw, gmb, seq, tile):
    n, d = x2.shape
    tpb = seq // tile
    row = lambda width: pl.BlockSpec((tile, width), lambda i: (i, 0))
    outs = [(256, BF16), (256, BF16), (256, BF16), (256, BF16), (128, F32), (256, F32), (256, F32), (512, F32)]
    return pl.pallas_call(
        _inproj_kernel,
        grid=(n // tile,),
        in_specs=[row(d),
                  pl.BlockSpec((None, 1, d), lambda i: (i // tpb, 0, 0)),
                  pl.BlockSpec((None, 1, d), lambda i: (i // tpb, 0, 1)),
                  _const_spec((d, D_IN_PACKED)),
                  row(LANES), row(LANES), row(LANES),
                  _const_spec((1, 256)), _const_spec((1, 256)),
                  _const_spec((GM_GROUPS, GM_CHUNK, GM_CHUNK)), _const_spec((GM_CHUNK, 256))],
        out_specs=[row(w) for w, _ in outs],
        out_shape=[jax.ShapeDtypeStruct((n, w), dt) for w, dt in outs],
        compiler_params=_cparams("parallel"),
        name="inproj",
    )(x2, mod, mod, w_in_p, rc, ra, rb, lng, lnb, gmw, gmb)


def _dsa_kernel(qi_ref, q_ref, aux_ref, kvk_ref, vt_ref, o_ref, key_ref, *, tq, kc, n_sel):
    j = pl.program_id(1)
    nh = ATT_HEADS
    n_kc = ((j + 1) * tq + kc - 1) // kc
    n_rows = n_kc * kc
    nt = (((1,), (1,)), ((), ()))

    qih = jnp.concatenate([qi_ref[:, h * IDX_DIM:(h + 1) * IDX_DIM] for h in range(IDX_HEADS)], axis=0)
    qh = jnp.concatenate([q_ref[:, h * ATT_HEAD_DIM:(h + 1) * ATT_HEAD_DIM] for h in range(nh)], axis=0)
    aux_t = aux_ref[...].T
    w_scale = IDX_HEADS ** -0.5 * IDX_DIM ** -0.5
    wrow = [aux_t[KV_WI - 128 + h:KV_WI - 128 + h + 1, :] * w_scale for h in range(IDX_HEADS)]

    qpos = j * tq + lax.broadcasted_iota(jnp.int32, (kc, tq), 1)
    krel = lax.broadcasted_iota(jnp.int32, (kc, tq), 0)

    def score_chunk(c, carry):
        k0 = pl.multiple_of(c * kc, kc)
        ki = kvk_ref[pl.ds(k0, kc), KV_KI:KV_KI + IDX_DIM]
        lg = lax.dot_general(ki, qih, nt, preferred_element_type=F32)
        sc = jnp.zeros((kc, tq), F32)
        for h in range(IDX_HEADS):
            sc = sc + jnp.maximum(lg[:, h * tq:(h + 1) * tq], 0.0) * wrow[h]
        sc = jnp.where(sc == 0.0, 0.0, sc)
        bits = lax.bitcast_convert_type(sc, jnp.int32)
        key = bits ^ ((bits >> 31) & jnp.int32(0x7FFFFFFF))
        key_ref[pl.ds(k0, kc), :] = jnp.where(k0 + krel <= qpos, key, jnp.int32(INT_MIN))
        return carry

    lax.fori_loop(0, n_kc, score_chunk, 0)

    rb = 64

    def count_ge(cand):
        cb = jnp.broadcast_to(cand, (rb, tq))

        def body(i, acc):
            blk = key_ref[pl.ds(pl.multiple_of(i * rb, rb), rb), :]
            return acc + (blk >= cb).astype(jnp.int32)

        acc = lax.fori_loop(0, n_rows // rb, body, jnp.zeros((rb, tq), jnp.int32))
        return jnp.sum(acc, axis=0, keepdims=True)

    zero_row = jnp.zeros((1, tq), jnp.int32)
    base = jnp.where(count_ge(zero_row) >= n_sel, zero_row, jnp.int32(INT_MIN))

    def bit_body(b, base):
        cand = base | jnp.left_shift(jnp.int32(1), 30 - b)
        return jnp.where(count_ge(cand) >= n_sel, cand, base)

    thr = lax.fori_loop(0, 31, bit_body, base)

    c_gt = count_ge(thr + 1)
    c_ge = count_ge(thr)
    need = n_sel - c_gt
    has_tie = jnp.max(jnp.where((c_ge > n_sel) & (thr > INT_MIN), 1, 0)) > 0
    tb = 256

    def flag_plain():
        def body(i, carry):
            sl = pl.ds(pl.multiple_of(i * tb, tb), tb)
            blk = key_ref[sl, :]
            key_ref[sl, :] = ((blk >= thr) & (blk > INT_MIN)).astype(jnp.int32)
            return carry
        lax.fori_loop(0, n_rows // tb, body, 0)

    def flag_ties():
        r_i = lax.broadcasted_iota(jnp.int32, (tb, tb), 0)
        c_i = lax.broadcasted_iota(jnp.int32, (tb, tb), 1)
        tri = (c_i < r_i).astype(BF16)
        need_f = need.astype(F32)

        def body(i, seen):
            sl = pl.ds(pl.multiple_of(i * tb, tb), tb)
            blk = key_ref[sl, :]
            eq = blk == thr
            before = jnp.dot(tri, eq.astype(BF16), preferred_element_type=F32) + seen
            keep = (blk > thr) | (eq & (before < need_f))
            key_ref[sl, :] = (keep & (blk > INT_MIN)).astype(jnp.int32)
            return seen + jnp.sum(eq.astype(F32), axis=0, keepdims=True)
        lax.fori_loop(0, n_rows // tb, body, jnp.zeros((1, tq), F32))

    lax.cond(has_tie, flag_ties, flag_plain)

    def att_chunk(c, carry):
        m, l, acc = carry
        k0 = pl.multiple_of(c * kc, kc)
        kk = kvk_ref[pl.ds(k0, kc), KV_K:KV_K + ATT_HEAD_DIM]
        lg = lax.dot_general(kk, qh, nt, preferred_element_type=F32)
        sel = key_ref[pl.ds(k0, kc), :] > 0
        m_parts, p_parts = [], []
        for h in range(nh):
            hs = slice(h * tq, (h + 1) * tq)
            lgh = jnp.where(sel, lg[:, hs], NEG_BIG)
            mh = jnp.maximum(m[:, hs], jnp.max(lgh, axis=0, keepdims=True))
            m_parts.append(mh)
            p_parts.append(jnp.where(sel, jnp.exp(lgh - mh), 0.0))
        m_new = jnp.concatenate(m_parts, axis=1)
        p = jnp.concatenate(p_parts, axis=1)
        alpha = jnp.exp(m - m_new)
        l = l * alpha + jnp.sum(p, axis=0, keepdims=True)
        acc = acc * alpha + jnp.dot(vt_ref[c], p.astype(BF16), preferred_element_type=F32)
        return m_new, l, acc

    m0 = jnp.full((1, nh * tq), NEG_BIG, F32)
    l0 = jnp.zeros((1, nh * tq), F32)
    a0 = jnp.zeros((ATT_HEAD_DIM, nh * tq), F32)
    _, l, acc = lax.fori_loop(0, n_kc, att_chunk, (m0, l0, a0))
    o = acc / l
    o = jnp.concatenate([o, jnp.zeros_like(o)], axis=0)
    for h in range(nh):
        o_ref[:, h * ATT_HEAD_DIM:(h + 1) * ATT_HEAD_DIM] = (
            o[:, h * tq:(h + 1) * tq].T[:, :ATT_HEAD_DIM].astype(o_ref.dtype))


def dsa(qi, q, aux, kvk, vt, batch, seq, tq=128, kc=512):
    n = qi.shape[0]
    nq = seq // tq
    n_sel = min(TOPK_MAX, seq // 4)
    blk = lambda w: pl.BlockSpec((tq, w), lambda b, j: (b * nq + j, 0))
    return pl.pallas_call(
        functools.partial(_dsa_kernel, tq=tq, kc=kc, n_sel=n_sel),
        grid=(batch, nq),
        in_specs=[blk(256), blk(256), blk(LANES),
                  pl.BlockSpec((seq, 256), lambda b, j: (b, 0), pipeline_mode=pl.Buffered(1)),
                  pl.BlockSpec((None, seq // kc, ATT_HEAD_DIM, kc), lambda b, j: (b, 0, 0, 0),
                               pipeline_mode=pl.Buffered(1))],
        out_specs=blk(256),
        out_shape=jax.ShapeDtypeStruct((n, 256), BF16),
        scratch_shapes=[pltpu.VMEM((seq, tq), jnp.int32)],
        compiler_params=_cparams("parallel", "arbitrary"),
        name="dsa",
    )(qi, q, aux, kvk, vt)


def s5_tables(lam_re, lam_im, log_step, b_re, b_im, c_re, c_im):
    hp = lax.Precision.HIGHEST
    L = S5_CHUNK
    g_, p_ = lam_re.shape
    step = jnp.exp(log_step)[:, None]
    d = jnp.arange(L + 1, dtype=F32)[None, :, None]
    mag = jnp.exp(d * (lam_re * step)[:, None, :])
    ang = d * (lam_im * step)[:, None, :]
    pw_re, pw_im = mag * jnp.cos(ang), mag * jnp.sin(ang)
    ab_re, ab_im = pw_re[:, 1], pw_im[:, 1]
    den = lam_re * lam_re + lam_im * lam_im
    nr = ab_re - 1.0
    cr = (nr * lam_re + ab_im * lam_im) / den
    ci = (ab_im * lam_re - nr * lam_im) / den
    bb_re = cr[..., None] * b_re - ci[..., None] * b_im
    bb_im = cr[..., None] * b_im + ci[..., None] * b_re
    cp_re = c_re[:, None] * pw_re[:, :, None, :] - c_im[:, None] * pw_im[:, :, None, :]
    cp_im = c_re[:, None] * pw_im[:, :, None, :] + c_im[:, None] * pw_re[:, :, None, :]
    kern = (jnp.einsum('gdop,gpi->gdoi', cp_re, bb_re, precision=hp)
            - jnp.einsum('gdop,gpi->gdoi', cp_im, bb_im, precision=hp))
    s_i = jnp.arange(L)[:, None]
    t_i = jnp.arange(L)[None, :]
    lag = t_i - s_i
    toe = jnp.where((lag >= 0)[None, :, :, None, None], kern[:, jnp.clip(lag, 0, L)], 0.0)
    toe = toe.transpose(0, 1, 4, 2, 3).reshape(g_, L * S5_GROUP, L * S5_GROUP)
    pw_rev_re, pw_rev_im = pw_re[:, L - 1::-1][:, :L], pw_im[:, L - 1::-1][:, :L]
    gm_re = pw_rev_re[:, :, None, :] * bb_re.transpose(0, 2, 1)[:, None] - pw_rev_im[:, :, None, :] * bb_im.transpose(0, 2, 1)[:, None]
    gm_im = pw_rev_re[:, :, None, :] * bb_im.transpose(0, 2, 1)[:, None] + pw_rev_im[:, :, None, :] * bb_re.transpose(0, 2, 1)[:, None]
    gm_re = gm_re.reshape(g_, L * S5_GROUP, p_)
    gm_im = gm_im.reshape(g_, L * S5_GROUP, p_)
    w = jnp.concatenate([toe, gm_re, gm_im], axis=2).astype(BF16)
    e_re = cp_re[:, 1:].transpose(0, 3, 1, 2).reshape(g_, p_, L * S5_GROUP)
    e_im = -cp_im[:, 1:].transpose(0, 3, 1, 2).reshape(g_, p_, L * S5_GROUP)
    e = jnp.concatenate([e_re, e_im], axis=1).astype(BF16)
    al = jnp.stack([pw_re[:, L], pw_im[:, L]], axis=1)
    return w, e, al


def _s5_kernel(u_ref, w_ref, e_ref, al_ref, y_ref, ys_ref, hp_ref, *, n_chunks, nb):
    lw = u_ref.shape[1]
    p_ = S5_STATE
    ys_ref[...] = jnp.dot(u_ref[...], w_ref[...], preferred_element_type=F32)
    ar = al_ref[0:1, :]
    ai = al_ref[1:2, :]

    rows_per_iter = nb * SUBLANES // math.gcd(nb, SUBLANES)
    cpi = rows_per_iter // nb

    def step(it, carry):
        hr, hi = carry
        rows = pl.ds(pl.multiple_of(it * rows_per_iter, rows_per_iter), rows_per_iter)
        st = ys_ref[rows, lw:lw + 2 * p_]
        hrs, his = [], []
        for k in range(cpi):
            hrs.append(hr)
            his.append(hi)
            sr, si = st[k * nb:(k + 1) * nb, 0:p_], st[k * nb:(k + 1) * nb, p_:2 * p_]
            hr, hi = ar * hr - ai * hi + sr, ar * hi + ai * hr + si
        hp_ref[rows, 0:p_] = jnp.concatenate(hrs, axis=0)
        hp_ref[rows, p_:2 * p_] = jnp.concatenate(his, axis=0)
        return hr, hi

    z = jnp.zeros((nb, p_), F32)
    lax.fori_loop(0, n_chunks // cpi, step, (z, z))
    y_ref[...] = ys_ref[:, 0:lw] + jnp.dot(hp_ref[...].astype(BF16), e_ref[...], preferred_element_type=F32)


def s5_scan(u_g, w, e, al, n_chunks, nb):
    g_, r, lw = u_g.shape
    return pl.pallas_call(
        functools.partial(_s5_kernel, n_chunks=n_chunks, nb=nb),
        grid=(g_,),
        in_specs=[pl.BlockSpec((None, r, lw), lambda g: (g, 0, 0)),
                  pl.BlockSpec((None, lw, lw + 2 * S5_STATE), lambda g: (g, 0, 0)),
                  pl.BlockSpec((None, 2 * S5_STATE, lw), lambda g: (g, 0, 0)),
                  pl.BlockSpec((None, 2, S5_STATE), lambda g: (g, 0, 0))],
        out_specs=pl.BlockSpec((None, r, lw), lambda g: (g, 0, 0)),
        out_shape=jax.ShapeDtypeStruct((g_, r, lw), F32),
        scratch_shapes=[pltpu.VMEM((r, lw + 2 * S5_STATE), F32), pltpu.VMEM((r, 2 * S5_STATE), F32)],
        compiler_params=_cparams("parallel"),
        name="s5",
    )(u_g, w, e, al)


def _ssd_kernel(z_ref, xbc_ref, aux_ref, cw_ref, cb_ref, dtb_ref, aneg_ref, dsk_ref, ng_ref,
                y_ref, state_ref, halo_ref, *, tile):
    hp = lax.Precision.HIGHEST
    ck = SSD_CHUNK
    hd = SSD_HEAD_DIM

    @pl.when(pl.program_id(1) == 0)
    def _():
        state_ref[...] = jnp.zeros_like(state_ref)
        halo_ref[...] = jnp.zeros_like(halo_ref)

    xbc = xbc_ref[...]
    ext = jnp.concatenate([halo_ref[...], xbc], axis=0)
    halo_ref[...] = xbc[tile - SUBLANES:tile, :]
    conv = cb_ref[...] + sum(cw_ref[jj:jj + 1, :] * ext[SUBLANES - (SSD_CONV - 1) + jj:SUBLANES - (SSD_CONV - 1) + jj + tile, :]
                             for jj in range(SSD_CONV))
    act = _silu(conv)
    x = act[:, 0:BRANCH_W]
    bm = act[:, BRANCH_W:BRANCH_W + SSD_GROUPS * SSD_STATE]
    cm = act[:, BRANCH_W + SSD_GROUPS * SSD_STATE:]
    dt_all = jax.nn.softplus(aux_ref[...] + dtb_ref[...])
    adt_all = dt_all * aneg_ref[...]

    r_i = lax.broadcasted_iota(jnp.int32, (ck, ck), 0)
    c_i = lax.broadcasted_iota(jnp.int32, (ck, ck), 1)
    lower = c_i <= r_i
    tri = lower.astype(F32)
    zs = _silu(z_ref[...])
    ng = ng_ref[...]
    dl = KV_DT - 128

    for ci in range(tile // ck):
        rows = slice(ci * ck, (ci + 1) * ck)
        a_cum = jnp.dot(tri, adt_all[rows], preferred_element_type=F32, precision=hp)
        a_cum_t = a_cum.T
        bm_t = bm[rows].T
        cmb = cm[rows].astype(BF16)
        bmb = bm[rows].astype(BF16)
        gmat = [lax.dot_general(cmb[:, g * SSD_STATE:(g + 1) * SSD_STATE], bmb[:, g * SSD_STATE:(g + 1) * SSD_STATE],
                                (((1,), (1,)), ((), ())), preferred_element_type=F32) for g in range(SSD_GROUPS)]
        ys = []
        for h in range(SSD_HEADS):
            g = h // (SSD_HEADS // SSD_GROUPS)
            acol = a_cum[:, dl + h:dl + h + 1]
            arow = a_cum_t[dl + h:dl + h + 1, :]
            dtc = dt_all[rows, dl + h:dl + h + 1]
            xh = x[rows, h * hd:(h + 1) * hd]
            xdt = xh * dtc
            lmat = jnp.exp(jnp.where(lower, acol - arow, NEG_BIG))
            y_diag = jnp.dot((gmat[g] * lmat).astype(BF16), xdt.astype(BF16), preferred_element_type=F32)
            s_prev = state_ref[h]
            y_off = jnp.exp(acol) * jnp.dot(cmb[:, g * SSD_STATE:(g + 1) * SSD_STATE], s_prev.astype(BF16),
                                            preferred_element_type=F32)
            a_last = a_cum[ck - 1:ck, dl + h:dl + h + 1]
            decay = jnp.exp(a_last - acol)
            s_new = jnp.dot(bm_t[g * SSD_STATE:(g + 1) * SSD_STATE, :].astype(BF16), (xdt * decay).astype(BF16),
                            preferred_element_type=F32)
            state_ref[h] = jnp.exp(a_last) * s_prev + s_new
            ys.append(y_diag + y_off + dsk_ref[:, h * hd:(h + 1) * hd] * xh)
        y = jnp.concatenate(ys, axis=1) * zs[rows]
        gw = BRANCH_W // SSD_GROUPS
        outs = []
        for g in range(SSD_GROUPS):
            yg = y[:, g * gw:(g + 1) * gw]
            outs.append(yg * lax.rsqrt(jnp.mean(yg * yg, axis=-1, keepdims=True) + RMS_EPS))
        y_ref[rows, :] = (jnp.concatenate(outs, axis=1) * ng).astype(y_ref.dtype)


def ssd(z, xbc, aux, cw, cb, dtb, aneg, dsk, ng, batch, seq, tile=512):
    n = z.shape[0]
    nt = seq // tile
    row = lambda w: pl.BlockSpec((tile, w), lambda b, i: (b * nt + i, 0))
    return pl.pallas_call(
        functools.partial(_ssd_kernel, tile=tile),
        grid=(batch, nt),
        in_specs=[row(256), row(512), row(LANES),
                  _const_spec((SSD_CONV, SSD_XBC)), _const_spec((1, SSD_XBC)), _const_spec((1, LANES)),
                  _const_spec((1, LANES)), _const_spec((1, 256)), _const_spec((1, 256))],
        out_specs=row(256),
        out_shape=jax.ShapeDtypeStruct((n, 256), BF16),
        scratch_shapes=[pltpu.VMEM((SSD_HEADS, SSD_STATE, SSD_HEAD_DIM), F32), pltpu.VMEM((SUBLANES, SSD_XBC), F32)],
        compiler_params=_cparams("parallel", "arbitrary"),
        name="ssd",
    )(z, xbc, aux, cw, cb, dtb, aneg, dsk, ng)


def _merge_kernel(x_ref, shift_ref, scale_ref, gate_ref, ya_ref, yb_ref, yc_ref, s5u_ref, yd_ref,
                  s5d_ref, gluw_ref, glub_ref, wg_ref, wb_ref, wo_ref, lng_ref, lnb_ref, o_ref):
    x = x_ref[...]
    h = (x * (1.0 + scale_ref[...]) + shift_ref[...]).astype(BF16)
    yc = jax.nn.gelu(yc_ref[...] + s5d_ref[...] * s5u_ref[...])
    yc = yc * _sigmoid(jnp.dot(yc.astype(BF16), gluw_ref[...], preferred_element_type=F32) + glub_ref[...])
    branches = (ya_ref[...], yb_ref[...], yc.astype(BF16), yd_ref[...])
    merged = None
    for i, br in enumerate(branches):
        gate = _sigmoid(jnp.dot(h, wg_ref[i], preferred_element_type=F32))
        term = gate * jnp.dot(br, wb_ref[i], preferred_element_type=F32)
        merged = term if merged is None else merged + term
    mix = jnp.dot(merged.astype(BF16), wo_ref[...], preferred_element_type=F32)
    o_ref[...] = _layer_norm(DN_ALPHA * x + (1.0 + gate_ref[...]) * mix, lng_ref[...], lnb_ref[...])


def merge(x2, mod, ya, yb, yc, s5u, yd, s5d, gluw, glub, wg, wb, wo, lng, lnb, seq, tile=512):
    n, d = x2.shape
    tpb = seq // tile
    row = lambda w: pl.BlockSpec((tile, w), lambda i: (i, 0))
    modspec = lambda k: pl.BlockSpec((None, 1, d), lambda i: (i // tpb, 0, k))
    return pl.pallas_call(
        _merge_kernel,
        grid=(n // tile,),
        in_specs=[row(d), modspec(0), modspec(1), modspec(2),
                  row(256), row(256), row(256), row(256), row(256),
                  _const_spec((1, 256)), _const_spec((256, 256)), _const_spec((1, 256)),
                  _const_spec((4, d, d)), _const_spec((4, 256, d)), _const_spec((d, d)),
                  _const_spec((1, d)), _const_spec((1, d))],
        out_specs=row(d),
        out_shape=jax.ShapeDtypeStruct((n, d), F32),
        compiler_params=_cparams("parallel"),
        name="merge",
    )(x2, mod, mod, mod, ya, yb, yc, s5u, yd, s5d, gluw, glub, wg, wb, wo, lng, lnb)


def _route(scores_t, bias_ref):
    ng = N_EXPERT_GROUPS
    per = N_EXPERTS // ng
    t = scores_t.shape[1]
    a = [scores_t[k * ng:(k + 1) * ng] for k in range(per)]
    b = [a[k] + bias_ref[k * ng:(k + 1) * ng, :] for k in range(per)]
    hi1, lo1 = jnp.maximum(b[0], b[1]), jnp.minimum(b[0], b[1])
    hi2, lo2 = jnp.maximum(b[2], b[3]), jnp.minimum(b[2], b[3])
    gs = jnp.maximum(hi1, hi2) + jnp.maximum(jnp.minimum(hi1, hi2), jnp.maximum(lo1, lo2))
    gidx = lax.broadcasted_iota(jnp.int32, (ng, t), 0)
    rank = jnp.zeros((ng, t), jnp.int32)
    for g2 in range(ng):
        row = gs[g2:g2 + 1, :]
        rank = rank + ((row > gs) | ((row == gs) & (g2 < gidx))).astype(jnp.int32)
    gmask = rank < TOPK_GROUPS
    neg_inf = jnp.float32(-jnp.inf)
    masked = [jnp.where(gmask, b[k], neg_inf) for k in range(per)]
    eidx = [gidx * per + k for k in range(per)]
    sel = [jnp.zeros((ng, t), jnp.bool_) for _ in range(per)]
    for _ in range(TOP_K):
        m = jnp.max(jnp.maximum(jnp.maximum(masked[0], masked[1]), jnp.maximum(masked[2], masked[3])),
                    axis=0, keepdims=True)
        cand = [jnp.where(masked[k] == m, eidx[k], N_EXPERTS) for k in range(per)]
        imin = jnp.min(jnp.minimum(jnp.minimum(cand[0], cand[1]), jnp.minimum(cand[2], cand[3])),
                       axis=0, keepdims=True)
        for k in range(per):
            hit = eidx[k] == imin
            sel[k] = sel[k] | hit
            masked[k] = jnp.where(hit, neg_inf, masked[k])
    w = [jnp.where(sel[k], a[k], 0.0) for k in range(per)]
    denom = jnp.sum(w[0] + w[1] + w[2] + w[3], axis=0, keepdims=True)
    gates = [w[k] / denom * ROUTED_SCALE for k in range(per)]
    return jnp.concatenate(gates + [jnp.zeros((LANES - N_EXPERTS, t), F32)], axis=0)


def _moe_kernel(x_ref, shift_ref, scale_ref, gate_ref, rw_ref, rb_ref, w13_ref, w2_ref, sw13_ref, sw2_ref,
                lng_ref, lnb_ref, o_ref, h_ref, acc_ref, g_ref):
    j = pl.program_id(1)
    ff = EXPERT_FF

    @pl.when(j == 0)
    def _():
        h = (x_ref[...] * (1.0 + scale_ref[...]) + shift_ref[...]).astype(BF16)
        h_ref[...] = h
        scores = _sigmoid(jnp.dot(h, rw_ref[...], preferred_element_type=F32))
        g_ref[...] = _route(scores.T, rb_ref).T
        ab = jnp.dot(h, sw13_ref[...], preferred_element_type=F32)
        act = _silu(ab[:, :ff]) * ab[:, ff:]
        acc_ref[...] = jnp.dot(act.astype(BF16), sw2_ref[...], preferred_element_type=F32)

    h = h_ref[...]
    ab = jnp.dot(h, w13_ref[...], preferred_element_type=F32)
    lane = lax.broadcasted_iota(jnp.int32, g_ref.shape, 1)
    gcol = jnp.sum(jnp.where(lane == j, g_ref[...], 0.0), axis=1, keepdims=True)
    act = _silu(ab[:, :ff]) * ab[:, ff:] * gcol
    acc_ref[...] += jnp.dot(act.astype(BF16), w2_ref[...], preferred_element_type=F32)

    @pl.when(j == pl.num_programs(1) - 1)
    def _():
        o_ref[...] = _layer_norm(DN_ALPHA * x_ref[...] + (1.0 + gate_ref[...]) * acc_ref[...],
                                 lng_ref[...], lnb_ref[...])


def moe(x2, mod, rw, rb, w13, w2, sw13, sw2, lng, lnb, seq, tile=1024):
    n, d = x2.shape
    tpb = seq // tile
    ne = w13.shape[0]
    per = N_EXPERTS // N_EXPERT_GROUPS
    emap = lambda j: (j % N_EXPERT_GROUPS) * per + j // N_EXPERT_GROUPS
    row = pl.BlockSpec((tile, d), lambda i, j: (i, 0))
    modspec = lambda k: pl.BlockSpec((None, 1, d), lambda i, j: (i // tpb, 0, k))
    return pl.pallas_call(
        _moe_kernel,
        grid=(n // tile, ne),
        in_specs=[row, modspec(0), modspec(1), modspec(2),
                  _const_spec((d, LANES)), _const_spec((N_EXPERTS, 1)),
                  pl.BlockSpec((None, d, 2 * EXPERT_FF), lambda i, j: (emap(j), 0, 0)),
                  pl.BlockSpec((None, EXPERT_FF, d), lambda i, j: (emap(j), 0, 0)),
                  _const_spec((d, 2 * EXPERT_FF)), _const_spec((EXPERT_FF, d)),
                  _const_spec((1, d)), _const_spec((1, d))],
        out_specs=row,
        out_shape=jax.ShapeDtypeStruct((n, d), F32),
        scratch_shapes=[pltpu.VMEM((tile, d), BF16), pltpu.VMEM((tile, d), F32), pltpu.VMEM((tile, LANES), F32)],
        compiler_params=_cparams("parallel", "arbitrary"),
        name="moe",
    )(x2, mod, mod, mod, rw, rb, w13, w2, sw13, sw2, lng, lnb)


def _pack_w_in(w):
    d = w.shape[0]
    cuts = np.cumsum([0, 256, 256, 256, 64, 64, 256, 64, 4, 256, 256, 512, 4])
    gm_u, gm_v, q, k, v, qi, ki, wi, s5, z, xbc, dt = [w[:, cuts[i]:cuts[i + 1]] for i in range(12)]
    pad = jnp.zeros((d, 256 - (64 * 3 + 8)), w.dtype)
    return jnp.concatenate([gm_u, gm_v, q, qi, k, v, ki, wi, dt, pad, s5, z, xbc], axis=1).astype(BF16)


def _rope_tables(positions):
    rot = 2 * ROPE_HALF
    inv = ROPE_THETA ** (-jnp.arange(0, rot, 2, dtype=F32) / rot)
    ang = positions.astype(F32).reshape(-1, 1) * inv
    cos, sin = jnp.cos(ang), jnp.sin(ang)
    n = ang.shape[0]
    rest = ATT_HEAD_DIM - rot
    one = jnp.ones((n, rest), F32)
    zero = jnp.zeros((n, rest), F32)
    z8 = jnp.zeros((n, ROPE_HALF), F32)
    rc = jnp.tile(jnp.concatenate([cos, cos, one], axis=1), (1, 2))
    ra = jnp.tile(jnp.concatenate([-sin, z8, zero], axis=1), (1, 2))
    rb = jnp.tile(jnp.concatenate([z8, sin, zero], axis=1), (1, 2))
    return rc, ra, rb


def _lane_pad(vec, offset):
    out = jnp.zeros((1, LANES), F32)
    return out.at[0, offset:offset + vec.shape[0]].set(vec.astype(F32))


def kernel(x, c, positions, mod1_w, mod1_b, w_in, gm_ln_g, gm_ln_b, gm_w, gm_b, s5_lam_re, s5_lam_im, s5_log_step, s5_b_re, s5_b_im, s5_c_re, s5_c_im, s5_d, s5_glu_w, s5_glu_b, ssd_conv_w, ssd_conv_b, ssd_dt_bias, ssd_a_log, ssd_d, ssd_norm_g, w_branch, w_gate, w_out, ln1_g, ln1_b, mod2_w, mod2_b, router_w, router_bias, exp_w1, exp_w3, exp_w2, sh_w1, sh_w3, sh_w2, ln2_g, ln2_b):
    bsz, seq, d = x.shape
    n = bsz * seq
    depth = w_in.shape[0]
    x2 = x.reshape(n, d)

    c_pad = jnp.zeros((SUBLANES, d), F32).at[:bsz].set(c)
    mods1 = adaln_all(c_pad, mod1_w, mod1_b[:, None, :])[:, :, None, :]
    mods2 = adaln_all(c_pad, mod2_w, mod2_b[:, None, :])[:, :, None, :]
    rc, ra, rb = _rope_tables(positions)

    tri = jnp.tril(jnp.ones((GM_CHUNK, GM_CHUNK), F32))
    per = N_EXPERTS // N_EXPERT_GROUPS
    eperm = np.array([(r % N_EXPERT_GROUPS) * per + r // N_EXPERT_GROUPS for r in range(N_EXPERTS)])
    kc = 512
    L = S5_CHUNK
    n_chunks = seq // L

    for l in range(depth):
        ya, q, qi, kvk, aux, s5u, z, xbc = inproj(
            x2, mods1[l], _pack_w_in(w_in[l]), rc, ra, rb,
            gm_ln_g[l][None], gm_ln_b[l][None], (gm_w[l] * tri).astype(BF16),
            jnp.repeat(gm_b[l].T, BRANCH_W // GM_GROUPS, axis=1), seq, 512)

        vt = kvk[:, KV_V:KV_V + ATT_HEAD_DIM].reshape(bsz, seq // kc, kc, ATT_HEAD_DIM).transpose(0, 1, 3, 2)
        yb = dsa(qi, q, aux, kvk, vt, bsz, seq, kc=kc)

        w5, e5, al5 = s5_tables(s5_lam_re[l], s5_lam_im[l], s5_log_step[l], s5_b_re[l], s5_b_im[l],
                                s5_c_re[l], s5_c_im[l])
        u_g = (s5u.astype(BF16).reshape(bsz, n_chunks, L, S5_GROUPS, S5_GROUP)
               .transpose(3, 1, 0, 2, 4).reshape(S5_GROUPS, n_chunks * bsz, L * S5_GROUP))
        y_g = s5_scan(u_g, w5, e5, al5, n_chunks, bsz)
        yc = (y_g.reshape(S5_GROUPS, n_chunks, bsz, L, S5_GROUP).transpose(2, 1, 3, 0, 4).reshape(n, BRANCH_W))

        yd = ssd(z, xbc, aux, ssd_conv_w[l], ssd_conv_b[l][None],
                 _lane_pad(ssd_dt_bias[l], KV_DT - 128), _lane_pad(-jnp.exp(ssd_a_log[l]), KV_DT - 128),
                 jnp.repeat(ssd_d[l], SSD_HEAD_DIM)[None], ssd_norm_g[l][None], bsz, seq)

        x2 = merge(x2, mods1[l], ya, yb, yc, s5u, yd, s5_d[l][None], s5_glu_w[l].astype(BF16), s5_glu_b[l][None],
                   w_gate[l].astype(BF16), w_branch[l].astype(BF16), w_out[l].astype(BF16),
                   ln1_g[l][None], ln1_b[l][None], seq)

        rw = jnp.zeros((d, LANES), F32).at[:, :N_EXPERTS].set(router_w[l][:, eperm]).astype(BF16)
        x2 = moe(x2, mods2[l], rw, router_bias[l][eperm][:, None],
                 jnp.concatenate([exp_w1[l], exp_w3[l]], axis=2).astype(BF16), exp_w2[l].astype(BF16),
                 jnp.concatenate([sh_w1[l], sh_w3[l]], axis=1).astype(BF16), sh_w2[l].astype(BF16),
                 ln2_g[l][None], ln2_b[l][None], seq)

    return x2.reshape(bsz, seq, d)
```

```python
import functools
import math

import jax
import jax.numpy as jnp
import numpy as np
from jax import lax
from jax.experimental import pallas as pl
from jax.experimental.pallas import tpu as pltpu

F32 = jnp.float32
BF16 = jnp.bfloat16

D_MODEL = 1024
DEPTH = 2
BRANCH_W = 256
GM_CHUNK = 128
GM_GROUPS = 4
ATT_HEAD_DIM = 64
ATT_HEADS = 4
IDX_HEADS = 4
IDX_DIM = 64
TOPK_MAX = 256
ROPE_THETA = 500000.0
ROPE_HALF = 8
S5_GROUP = 16
S5_GROUPS = 16
S5_STATE = 64
S5_CHUNK = 32
SSD_HEAD_DIM = 64
SSD_HEADS = 4
SSD_GROUPS = 2
SSD_STATE = 64
SSD_CONV = 4
SSD_CHUNK = 128
SSD_XBC = 512
N_EXPERTS = 32
TOP_K = 8
N_EXPERT_GROUPS = 8
TOPK_GROUPS = 4
EXPERT_FF = 256
ROUTED_SCALE = 2.5
DN_ALPHA = (2 * DEPTH) ** 0.25
LN_EPS = 1e-5
RMS_EPS = 1e-6

LANES = 128
SUBLANES = 8
VMEM_LIMIT_BYTES = 56 * 1024 * 1024

C_GMU, C_GMV, C_Q, C_QI, C_KV, C_S5, C_Z, C_XBC = 0, 256, 512, 768, 1024, 1280, 1536, 1792
D_IN_PACKED = 2304
KV_K, KV_V, KV_KI, KV_WI, KV_DT = 0, 64, 128, 192, 196

NEG_BIG = -1e30
INT_MIN = -2 ** 31


def _cparams(*sem):
    return pltpu.CompilerParams(dimension_semantics=sem, vmem_limit_bytes=VMEM_LIMIT_BYTES)


def _const_spec(shape):
    nd = len(shape)
    return pl.BlockSpec(shape, lambda *_: (0,) * nd, pipeline_mode=pl.Buffered(1))


def _layer_norm(v, g, b):
    mu = jnp.mean(v, axis=-1, keepdims=True)
    d = v - mu
    var = jnp.mean(d * d, axis=-1, keepdims=True)
    return d * lax.rsqrt(var + LN_EPS) * g + b


def _sigmoid(v):
    return 1.0 / (1.0 + jnp.exp(-v))


def _silu(v):
    return v * _sigmoid(v)


def _adaln_kernel(c_ref, w_ref, b_ref, o_ref):
    o_ref[...] = jnp.dot(c_ref[...], w_ref[...], preferred_element_type=F32,
                         precision=lax.Precision.HIGHEST) + b_ref[...]


def adaln_all(c_pad, w_all, b_all):
    m, d, d3 = w_all.shape
    bn = 512
    return pl.pallas_call(
        _adaln_kernel,
        grid=(m, d3 // bn),
        in_specs=[pl.BlockSpec((SUBLANES, d), lambda i, j: (0, 0)),
                  pl.BlockSpec((None, d, bn), lambda i, j: (i, 0, j)),
                  pl.BlockSpec((None, 1, bn), lambda i, j: (i, 0, j))],
        out_specs=pl.BlockSpec((None, SUBLANES, bn), lambda i, j: (i, 0, j)),
        out_shape=jax.ShapeDtypeStruct((m, SUBLANES, d3), F32),
        compiler_params=_cparams("parallel", "parallel"),
        name="adaln",
    )(c_pad, w_all, b_all)


def _rope128(t, c, a, b):
    return t * c + pltpu.roll(t, LANES - ROPE_HALF, axis=1) * a + pltpu.roll(t, ROPE_HALF, axis=1) * b


def _inproj_kernel(x_ref, shift_ref, scale_ref, w_ref, rc_ref, ra_ref, rb_ref,
                   lng_ref, lnb_ref, gmw_ref, gmb_ref,
                   ya_ref, q_ref, qi_ref, kvk_ref, aux_ref, s5u_ref, z_ref, xbc_ref):
    t = x_ref.shape[0]
    h = (x_ref[...] * (1.0 + scale_ref[...]) + shift_ref[...]).astype(BF16)
    rc, ra, rb = rc_ref[...], ra_ref[...], rb_ref[...]

    def proj(c0, width):
        return jnp.dot(h, w_ref[:, c0:c0 + width], preferred_element_type=F32)

    for c0, o_ref, mul in ((C_Q, q_ref, ATT_HEAD_DIM ** -0.5 * math.log2(math.e)), (C_QI, qi_ref, 1.0)):
        p = proj(c0, 256)
        o_ref[:, 0:128] = (_rope128(p[:, 0:128], rc, ra, rb) * mul).astype(BF16)
        o_ref[:, 128:256] = (_rope128(p[:, 128:256], rc, ra, rb) * mul).astype(BF16)

    p = proj(C_KV, 256)
    lane = lax.broadcasted_iota(jnp.int32, (t, LANES), 1)
    first = lane < ATT_HEAD_DIM
    rc1 = jnp.where(first, rc, 1.0)
    ra1 = jnp.where(first, ra, 0.0)
    rb1 = jnp.where(first, rb, 0.0)
    kv = _rope128(p[:, 0:128], rc1, ra1, rb1)
    kiw = _rope128(p[:, 128:256], rc1, ra1, rb1)
    kvk_ref[:, 0:128] = kv.astype(BF16)
    kvk_ref[:, 128:256] = kiw.astype(BF16)
    aux_ref[...] = kiw

    s5u_ref[...] = proj(C_S5, 256)
    z_ref[...] = proj(C_Z, 256)
    xbc_ref[...] = proj(C_XBC, 512)

    u = jax.nn.gelu(proj(C_GMU, 256))
    v = _layer_norm(jax.nn.gelu(proj(C_GMV, 256)), lng_ref[...], lnb_ref[...]).astype(BF16)
    gw = GM_CHUNK
    cw = BRANCH_W // GM_GROUPS
    for ci in range(t // gw):
        rows = slice(ci * gw, (ci + 1) * gw)
        for g in range(GM_GROUPS):
            cols = slice(g * cw, (g + 1) * cw)
            sv = jnp.dot(gmw_ref[g], v[rows, cols], preferred_element_type=F32) + gmb_ref[:, cols]
            ya_ref[rows, cols] = (u[rows, cols] * sv).astype(BF16)


def inproj(x2, mod, w_in_p, rc, ra, rb, lng, lnb, gmw, gmb, seq, tile):
    n, d = x2.shape
    tpb = seq // tile
    row = lambda width: pl.BlockSpec((tile, width), lambda i: (i, 0))
    outs = [(256, BF16), (256, BF16), (256, BF16), (256, BF16), (128, F32), (256, F32), (256, F32), (512, F32)]
    return pl.pallas_call(
        _inproj_kernel,
        grid=(n // tile,),
        in_specs=[row(d),
                  pl.BlockSpec((None, 1, d), lambda i: (i // tpb, 0, 0)),
                  pl.BlockSpec((None, 1, d), lambda i: (i // tpb, 0, 1)),
                  _const_spec((d, D_IN_PACKED)),
                  row(LANES), row(LANES), row(LANES),
                  _const_spec((1, 256)), _const_spec((1, 256)),
                  _const_spec((GM_GROUPS, GM_CHUNK, GM_CHUNK)), _const_spec((GM_CHUNK, 256))],
        out_specs=[row(w) for w, _ in outs],
        out_shape=[jax.ShapeDtypeStruct((n, w), dt) for w, dt in outs],
        compiler_params=_cparams("parallel"),
        name="inproj",
    )(x2, mod, mod, w_in_p, rc, ra, rb, lng, lnb, gmw, gmb)


def _dsa_kernel(qi_ref, q_ref, aux_ref, kvk_ref, vt_ref, o_ref, key_ref, planes_ref, mask_ref, lg0_ref, lg1_ref,
                *, tq, kc, n_sel):
    j = pl.program_id(1)
    nh = ATT_HEADS
    n_it = ((j + 1) * tq + 2 * kc - 1) // (2 * kc)
    n_rows = n_it * 2 * kc
    nt = (((1,), (1,)), ((), ()))

    qih = jnp.concatenate([qi_ref[:, h * IDX_DIM:(h + 1) * IDX_DIM] for h in range(IDX_HEADS)], axis=0)
    qh = jnp.concatenate([q_ref[:, h * ATT_HEAD_DIM:(h + 1) * ATT_HEAD_DIM] for h in range(nh)], axis=0)
    aux_t = aux_ref[...].T
    w_scale = IDX_HEADS ** -0.5 * IDX_DIM ** -0.5
    wrow = [aux_t[KV_WI - 128 + h:KV_WI - 128 + h + 1, :] * w_scale for h in range(IDX_HEADS)]

    qpos = j * tq + lax.broadcasted_iota(jnp.int32, (kc, tq), 1)
    krel = lax.broadcasted_iota(jnp.int32, (kc, tq), 0)

    last_k0 = kvk_ref.shape[0] - kc

    def logits(col0, q_rows, k0):
        kk = kvk_ref[pl.ds(pl.multiple_of(jnp.minimum(k0, last_k0), kc), kc), col0:col0 + IDX_DIM]
        return lax.dot_general(kk, q_rows, nt, preferred_element_type=F32)

    def pipelined(col0, q_rows, consume, carry):
        lg0_ref[...] = logits(col0, q_rows, 0)

        def pair(i, carry):
            k0 = pl.multiple_of(i * 2 * kc, 2 * kc)
            lg1_ref[...] = logits(col0, q_rows, k0 + kc)
            carry = consume(lg0_ref, k0, 2 * i, carry)
            lg0_ref[...] = logits(col0, q_rows, k0 + 2 * kc)
            return consume(lg1_ref, k0 + kc, 2 * i + 1, carry)

        return lax.fori_loop(0, n_it, pair, carry)

    def score_chunk(lg_ref, k0, c, carry):
        sc = jnp.zeros((kc, tq), F32)
        for h in range(IDX_HEADS):
            sc = sc + jnp.maximum(lg_ref[:, h * tq:(h + 1) * tq], 0.0) * wrow[h]
        sc = jnp.where(sc == 0.0, 0.0, sc)
        bits = lax.bitcast_convert_type(sc, jnp.int32)
        key = bits ^ ((bits >> 31) & jnp.int32(0x7FFFFFFF))
        key_ref[pl.ds(k0, kc), :] = jnp.where(k0 + krel <= qpos, key, jnp.int32(INT_MIN))
        return carry

    pipelined(KV_KI, qih, score_chunk, 0)

    wb = 32
    sbr = wb * SUBLANES

    def to_planes(sb, carry):
        r0 = pl.multiple_of(sb * sbr, sbr)
        a = [key_ref[pl.ds(r0 + SUBLANES * w, SUBLANES), :] ^ jnp.int32(INT_MIN) for w in range(wb)]
        half, m = wb // 2, 0x0000FFFF
        while half:
            k = 0
            while k < wb:
                t = (a[k] ^ lax.shift_right_logical(a[k + half], half)) & jnp.int32(np.uint32(m).astype(np.int32))
                a[k] = a[k] ^ t
                a[k + half] = a[k + half] ^ (t << half)
                k = (k + half + 1) & ~half
            half >>= 1
            m ^= (m << half) & 0xFFFFFFFF
        rows = pl.ds(pl.multiple_of(sb * SUBLANES, SUBLANES), SUBLANES)
        for p in range(wb):
            planes_ref[p, rows, :] = a[p]
        return carry

    lax.fori_loop(0, n_rows // sbr, to_planes, 0)

    pr = planes_ref.shape[1]
    prow = lax.broadcasted_iota(jnp.int32, (pr, tq), 0)
    live0 = jnp.where(prow < n_rows // wb, jnp.int32(-1), jnp.int32(0))

    def radix_step(live, remaining, plane):
        ones = jnp.sum(lax.population_count(live & plane), axis=0, keepdims=True)
        take = ones >= remaining
        remaining = jnp.where(take, remaining, remaining - ones)
        live = live & (plane ^ jnp.where(take, 0, -1))
        return live, remaining, take

    def score_pass(p, carry):
        live, remaining, thr_u = carry
        live, remaining, take = radix_step(live, remaining, planes_ref[p])
        thr_u = thr_u | jnp.where(take, lax.shift_right_logical(jnp.int32(INT_MIN), p), 0)
        return live, remaining, thr_u

    live, remaining, thr_u = lax.fori_loop(
        0, wb, score_pass, (live0, jnp.full((1, tq), n_sel, jnp.int32), jnp.zeros((1, tq), jnp.int32)))
    thr = thr_u ^ jnp.int32(INT_MIN)

    n_tiles = pr // SUBLANES
    tile_inv = (n_tiles - 1) - (prow >> 3)
    sub_inv = (SUBLANES - 1) - (prow & (SUBLANES - 1))
    lane_bit = [0xAAAAAAAA, 0xCCCCCCCC, 0xF0F0F0F0, 0xFF00FF00, 0xFFFF0000]
    inv_planes = ([(8 + t, -((tile_inv >> t) & 1)) for t in range(max(n_tiles - 1, 1).bit_length())]
                  + [(3 + t, jnp.full((pr, tq), np.uint32(lane_bit[t]).astype(np.int32), jnp.int32)) for t in range(5)]
                  + [(t, -((sub_inv >> t) & 1)) for t in range(3)])
    inv_cut = jnp.zeros((1, tq), jnp.int32)
    for bit, plane in sorted(inv_planes, key=lambda e: -e[0]):
        live, remaining, take = radix_step(live, remaining, plane)
        inv_cut = inv_cut | jnp.where(take, 1 << bit, 0)
    row_cut = (pr * wb - 1) - inv_cut

    tb = 256
    rrel = lax.broadcasted_iota(jnp.int32, (tb, tq), 0)

    def flag_rows(i, carry):
        r0 = pl.multiple_of(i * tb, tb)
        blk = key_ref[pl.ds(r0, tb), :]
        keep = ((blk > thr) | ((blk == thr) & (r0 + rrel <= row_cut))) & (blk > INT_MIN)
        mask_ref[pl.ds(r0, tb), :] = jnp.where(keep, 0.0, NEG_BIG)
        return carry

    lax.fori_loop(0, n_rows // tb, flag_rows, 0)

    def att_chunk(lg_ref, k0, c, carry):
        m, l, acc = carry
        madd = mask_ref[pl.ds(k0, kc), :]
        m_parts, p_parts = [], []
        for h in range(nh):
            hs = slice(h * tq, (h + 1) * tq)
            lgh = lg_ref[:, hs] + madd
            mh = jnp.maximum(m[:, hs], jnp.max(lgh, axis=0, keepdims=True))
            m_parts.append(mh)
            p_parts.append(jnp.exp2(lgh - mh))
        m_new = jnp.concatenate(m_parts, axis=1)
        p = jnp.concatenate(p_parts, axis=1)
        alpha = jnp.exp2(m - m_new)
        l = l * alpha + jnp.sum(p, axis=0, keepdims=True)
        acc = acc * alpha + jnp.dot(vt_ref[c], p.astype(BF16), preferred_element_type=F32)
        return m_new, l, acc

    m0 = jnp.full((1, nh * tq), NEG_BIG, F32)
    l0 = jnp.zeros((1, nh * tq), F32)
    a0 = jnp.zeros((ATT_HEAD_DIM, nh * tq), F32)
    _, l, acc = pipelined(KV_K, qh, att_chunk, (m0, l0, a0))
    o = acc / l
    o = jnp.concatenate([o, jnp.zeros_like(o)], axis=0)
    for h in range(nh):
        o_ref[:, h * ATT_HEAD_DIM:(h + 1) * ATT_HEAD_DIM] = (
            o[:, h * tq:(h + 1) * tq].T[:, :ATT_HEAD_DIM].astype(o_ref.dtype))


DSA_KEY_CHUNK = 256


def dsa(qi, q, aux, kvk, vt, batch, seq, tq=128, kc=DSA_KEY_CHUNK):
    n = qi.shape[0]
    nq = seq // tq
    n_sel = min(TOPK_MAX, seq // 4)
    blk = lambda w: pl.BlockSpec((tq, w), lambda b, j: (b * nq + j, 0))
    return pl.pallas_call(
        functools.partial(_dsa_kernel, tq=tq, kc=kc, n_sel=n_sel),
        grid=(batch, nq),
        in_specs=[blk(256), blk(256), blk(LANES),
                  pl.BlockSpec((seq, 256), lambda b, j: (b, 0), pipeline_mode=pl.Buffered(1)),
                  pl.BlockSpec((None, seq // kc, ATT_HEAD_DIM, kc), lambda b, j: (b, 0, 0, 0),
                               pipeline_mode=pl.Buffered(1))],
        out_specs=blk(256),
        out_shape=jax.ShapeDtypeStruct((n, 256), BF16),
        scratch_shapes=[pltpu.VMEM((seq, tq), jnp.int32),
                        pltpu.VMEM((32, seq // 32, tq), jnp.int32),
                        pltpu.VMEM((seq, tq), F32),
                        pltpu.VMEM((kc, ATT_HEADS * tq), F32),
                        pltpu.VMEM((kc, ATT_HEADS * tq), F32)],
        compiler_params=_cparams("parallel", "arbitrary"),
        name="dsa",
    )(qi, q, aux, kvk, vt)


def s5_tables(lam_re, lam_im, log_step, b_re, b_im, c_re, c_im):
    hp = lax.Precision.HIGHEST
    L = S5_CHUNK
    g_, p_ = lam_re.shape
    step = jnp.exp(log_step)[:, None]
    d = jnp.arange(L + 1, dtype=F32)[None, :, None]
    mag = jnp.exp(d * (lam_re * step)[:, None, :])
    ang = d * (lam_im * step)[:, None, :]
    pw_re, pw_im = mag * jnp.cos(ang), mag * jnp.sin(ang)
    ab_re, ab_im = pw_re[:, 1], pw_im[:, 1]
    den = lam_re * lam_re + lam_im * lam_im
    nr = ab_re - 1.0
    cr = (nr * lam_re + ab_im * lam_im) / den
    ci = (ab_im * lam_re - nr * lam_im) / den
    bb_re = cr[..., None] * b_re - ci[..., None] * b_im
    bb_im = cr[..., None] * b_im + ci[..., None] * b_re
    cp_re = c_re[:, None] * pw_re[:, :, None, :] - c_im[:, None] * pw_im[:, :, None, :]
    cp_im = c_re[:, None] * pw_im[:, :, None, :] + c_im[:, None] * pw_re[:, :, None, :]
    kern = (jnp.einsum('gdop,gpi->gdoi', cp_re, bb_re, precision=hp)
            - jnp.einsum('gdop,gpi->gdoi', cp_im, bb_im, precision=hp))
    s_i = jnp.arange(L)[:, None]
    t_i = jnp.arange(L)[None, :]
    lag_sel = (t_i - s_i == jnp.arange(L)[:, None, None]).astype(F32)
    toe = jnp.einsum('gdoi,dst->gsito', kern[:, :L], lag_sel, precision=hp)
    toe = toe.reshape(g_, L * S5_GROUP, L * S5_GROUP)
    pw_rev_re, pw_rev_im = pw_re[:, L - 1::-1][:, :L], pw_im[:, L - 1::-1][:, :L]
    gm_re = pw_rev_re[:, :, None, :] * bb_re.transpose(0, 2, 1)[:, None] - pw_rev_im[:, :, None, :] * bb_im.transpose(0, 2, 1)[:, None]
    gm_im = pw_rev_re[:, :, None, :] * bb_im.transpose(0, 2, 1)[:, None] + pw_rev_im[:, :, None, :] * bb_re.transpose(0, 2, 1)[:, None]
    gm_re = gm_re.reshape(g_, L * S5_GROUP, p_)
    gm_im = gm_im.reshape(g_, L * S5_GROUP, p_)
    w = jnp.concatenate([toe, gm_re, gm_im], axis=2).astype(BF16)
    e_re = cp_re[:, 1:].transpose(0, 3, 1, 2).reshape(g_, p_, L * S5_GROUP)
    e_im = -cp_im[:, 1:].transpose(0, 3, 1, 2).reshape(g_, p_, L * S5_GROUP)
    e = jnp.concatenate([e_re, e_im], axis=1).astype(BF16)
    al = jnp.stack([pw_re[:, L], pw_im[:, L]], axis=1)
    return w, e, al


def _s5_kernel(u_ref, w_ref, e_ref, al_ref, y_ref, ys_ref, hp_ref, *, n_chunks, nb):
    lw = u_ref.shape[1]
    p_ = S5_STATE
    ys_ref[...] = jnp.dot(u_ref[...], w_ref[...], preferred_element_type=F32)
    ar = al_ref[0:1, :]
    ai = al_ref[1:2, :]

    rows_per_iter = nb * SUBLANES // math.gcd(nb, SUBLANES)
    cpi = rows_per_iter // nb

    def step(it, carry):
        hr, hi = carry
        rows = pl.ds(pl.multiple_of(it * rows_per_iter, rows_per_iter), rows_per_iter)
        st = ys_ref[rows, lw:lw + 2 * p_]
        hrs, his = [], []
        for k in range(cpi):
            hrs.append(hr)
            his.append(hi)
            sr, si = st[k * nb:(k + 1) * nb, 0:p_], st[k * nb:(k + 1) * nb, p_:2 * p_]
            hr, hi = ar * hr - ai * hi + sr, ar * hi + ai * hr + si
        hp_ref[rows, 0:p_] = jnp.concatenate(hrs, axis=0)
        hp_ref[rows, p_:2 * p_] = jnp.concatenate(his, axis=0)
        return hr, hi

    z = jnp.zeros((nb, p_), F32)
    lax.fori_loop(0, n_chunks // cpi, step, (z, z))
    y_ref[...] = (ys_ref[:, 0:lw] + jnp.dot(hp_ref[...].astype(BF16), e_ref[...], preferred_element_type=F32)
                  ).astype(y_ref.dtype)


def s5_scan(u_g, w, e, al, n_chunks, nb):
    g_, r, lw = u_g.shape
    return pl.pallas_call(
        functools.partial(_s5_kernel, n_chunks=n_chunks, nb=nb),
        grid=(g_,),
        in_specs=[pl.BlockSpec((None, r, lw), lambda g: (g, 0, 0)),
                  pl.BlockSpec((None, lw, lw + 2 * S5_STATE), lambda g: (g, 0, 0)),
                  pl.BlockSpec((None, 2 * S5_STATE, lw), lambda g: (g, 0, 0)),
                  pl.BlockSpec((None, 2, S5_STATE), lambda g: (g, 0, 0))],
        out_specs=pl.BlockSpec((None, r, lw), lambda g: (g, 0, 0)),
        out_shape=jax.ShapeDtypeStruct((g_, r, lw), BF16),
        scratch_shapes=[pltpu.VMEM((r, lw + 2 * S5_STATE), F32), pltpu.VMEM((r, 2 * S5_STATE), F32)],
        compiler_params=_cparams("parallel"),
        name="s5",
    )(u_g, w, e, al)


def _ssd_kernel(z_ref, xbc_ref, aux_ref, cw_ref, cb_ref, dtb_ref, aneg_ref, dsk_ref, ng_ref,
                y_ref, state_ref, halo_ref, *, tile):
    hp = lax.Precision.HIGHEST
    ck = SSD_CHUNK
    hd = SSD_HEAD_DIM

    @pl.when(pl.program_id(1) == 0)
    def _():
        state_ref[...] = jnp.zeros_like(state_ref)
        halo_ref[...] = jnp.zeros_like(halo_ref)

    xbc = xbc_ref[...]
    ext = jnp.concatenate([halo_ref[...], xbc], axis=0)
    halo_ref[...] = xbc[tile - SUBLANES:tile, :]
    conv = cb_ref[...] + sum(cw_ref[jj:jj + 1, :] * ext[SUBLANES - (SSD_CONV - 1) + jj:SUBLANES - (SSD_CONV - 1) + jj + tile, :]
                             for jj in range(SSD_CONV))
    act = _silu(conv)
    x = act[:, 0:BRANCH_W]
    bm = act[:, BRANCH_W:BRANCH_W + SSD_GROUPS * SSD_STATE]
    cm = act[:, BRANCH_W + SSD_GROUPS * SSD_STATE:]
    dt_all = jax.nn.softplus(aux_ref[...] + dtb_ref[...])
    adt_all = dt_all * aneg_ref[...]

    r_i = lax.broadcasted_iota(jnp.int32, (ck, ck), 0)
    c_i = lax.broadcasted_iota(jnp.int32, (ck, ck), 1)
    lower = c_i <= r_i
    tri = lower.astype(F32)
    zs = _silu(z_ref[...])
    ng = ng_ref[...]
    dl = KV_DT - 128

    for ci in range(tile // ck):
        rows = slice(ci * ck, (ci + 1) * ck)
        a_cum = jnp.dot(tri, adt_all[rows], preferred_element_type=F32, precision=hp)
        a_cum_t = a_cum.T
        bm_t = bm[rows].T
        cmb = cm[rows].astype(BF16)
        bmb = bm[rows].astype(BF16)
        gmat = [lax.dot_general(cmb[:, g * SSD_STATE:(g + 1) * SSD_STATE], bmb[:, g * SSD_STATE:(g + 1) * SSD_STATE],
                                (((1,), (1,)), ((), ())), preferred_element_type=F32) for g in range(SSD_GROUPS)]
        ys = []
        for h in range(SSD_HEADS):
            g = h // (SSD_HEADS // SSD_GROUPS)
            acol = a_cum[:, dl + h:dl + h + 1]
            arow = a_cum_t[dl + h:dl + h + 1, :]
            dtc = dt_all[rows, dl + h:dl + h + 1]
            xh = x[rows, h * hd:(h + 1) * hd]
            xdt = xh * dtc
            lmat = jnp.exp(jnp.where(lower, acol - arow, NEG_BIG))
            y_diag = jnp.dot((gmat[g] * lmat).astype(BF16), xdt.astype(BF16), preferred_element_type=F32)
            s_prev = state_ref[h]
            y_off = jnp.exp(acol) * jnp.dot(cmb[:, g * SSD_STATE:(g + 1) * SSD_STATE], s_prev.astype(BF16),
                                            preferred_element_type=F32)
            a_last = a_cum[ck - 1:ck, dl + h:dl + h + 1]
            decay = jnp.exp(a_last - acol)
            s_new = jnp.dot(bm_t[g * SSD_STATE:(g + 1) * SSD_STATE, :].astype(BF16), (xdt * decay).astype(BF16),
                            preferred_element_type=F32)
            state_ref[h] = jnp.exp(a_last) * s_prev + s_new
            ys.append(y_diag + y_off + dsk_ref[:, h * hd:(h + 1) * hd] * xh)
        y = jnp.concatenate(ys, axis=1) * zs[rows]
        gw = BRANCH_W // SSD_GROUPS
        outs = []
        for g in range(SSD_GROUPS):
            yg = y[:, g * gw:(g + 1) * gw]
            outs.append(yg * lax.rsqrt(jnp.mean(yg * yg, axis=-1, keepdims=True) + RMS_EPS))
        y_ref[rows, :] = (jnp.concatenate(outs, axis=1) * ng).astype(y_ref.dtype)


def ssd(z, xbc, aux, cw, cb, dtb, aneg, dsk, ng, batch, seq, tile=512):
    n = z.shape[0]
    nt = seq // tile
    row = lambda w: pl.BlockSpec((tile, w), lambda b, i: (b * nt + i, 0))
    return pl.pallas_call(
        functools.partial(_ssd_kernel, tile=tile),
        grid=(batch, nt),
        in_specs=[row(256), row(512), row(LANES),
                  _const_spec((SSD_CONV, SSD_XBC)), _const_spec((1, SSD_XBC)), _const_spec((1, LANES)),
                  _const_spec((1, LANES)), _const_spec((1, 256)), _const_spec((1, 256))],
        out_specs=row(256),
        out_shape=jax.ShapeDtypeStruct((n, 256), BF16),
        scratch_shapes=[pltpu.VMEM((SSD_HEADS, SSD_STATE, SSD_HEAD_DIM), F32), pltpu.VMEM((SUBLANES, SSD_XBC), F32)],
        compiler_params=_cparams("parallel", "arbitrary"),
        name="ssd",
    )(z, xbc, aux, cw, cb, dtb, aneg, dsk, ng)


def _merge_kernel(x_ref, shift_ref, scale_ref, gate_ref, ya_ref, yb_ref, yc_ref, s5u_ref, yd_ref,
                  s5d_ref, gluw_ref, glub_ref, wg_ref, wb_ref, wo_ref, lng_ref, lnb_ref, o_ref):
    x = x_ref[...]
    h = (x * (1.0 + scale_ref[...]) + shift_ref[...]).astype(BF16)
    yc = jax.nn.gelu(yc_ref[...] + s5d_ref[...] * s5u_ref[...])
    yc = yc * _sigmoid(jnp.dot(yc.astype(BF16), gluw_ref[...], preferred_element_type=F32) + glub_ref[...])
    branches = (ya_ref[...], yb_ref[...], yc.astype(BF16), yd_ref[...])
    merged = None
    for i, br in enumerate(branches):
        gate = _sigmoid(jnp.dot(h, wg_ref[i], preferred_element_type=F32))
        term = gate * jnp.dot(br, wb_ref[i], preferred_element_type=F32)
        merged = term if merged is None else merged + term
    mix = jnp.dot(merged.astype(BF16), wo_ref[...], preferred_element_type=F32)
    o_ref[...] = _layer_norm(DN_ALPHA * x + (1.0 + gate_ref[...]) * mix, lng_ref[...], lnb_ref[...])


def merge(x2, mod, ya, yb, yc, s5u, yd, s5d, gluw, glub, wg, wb, wo, lng, lnb, seq, tile=512):
    n, d = x2.shape
    tpb = seq // tile
    row = lambda w: pl.BlockSpec((tile, w), lambda i: (i, 0))
    modspec = lambda k: pl.BlockSpec((None, 1, d), lambda i: (i // tpb, 0, k))
    return pl.pallas_call(
        _merge_kernel,
        grid=(n // tile,),
        in_specs=[row(d), modspec(0), modspec(1), modspec(2),
                  row(256), row(256), row(256), row(256), row(256),
                  _const_spec((1, 256)), _const_spec((256, 256)), _const_spec((1, 256)),
                  _const_spec((4, d, d)), _const_spec((4, 256, d)), _const_spec((d, d)),
                  _const_spec((1, d)), _const_spec((1, d))],
        out_specs=row(d),
        out_shape=jax.ShapeDtypeStruct((n, d), F32),
        compiler_params=_cparams("parallel"),
        name="merge",
    )(x2, mod, mod, mod, ya, yb, yc, s5u, yd, s5d, gluw, glub, wg, wb, wo, lng, lnb)


def _route(scores_t, bias_ref):
    ng = N_EXPERT_GROUPS
    per = N_EXPERTS // ng
    t = scores_t.shape[1]
    a = [scores_t[k * ng:(k + 1) * ng] for k in range(per)]
    b = [a[k] + bias_ref[k * ng:(k + 1) * ng, :] for k in range(per)]
    hi1, lo1 = jnp.maximum(b[0], b[1]), jnp.minimum(b[0], b[1])
    hi2, lo2 = jnp.maximum(b[2], b[3]), jnp.minimum(b[2], b[3])
    gs = jnp.maximum(hi1, hi2) + jnp.maximum(jnp.minimum(hi1, hi2), jnp.maximum(lo1, lo2))
    gidx = lax.broadcasted_iota(jnp.int32, (ng, t), 0)
    rank = jnp.zeros((ng, t), jnp.int32)
    for g2 in range(ng):
        row = gs[g2:g2 + 1, :]
        rank = rank + ((row > gs) | ((row == gs) & (g2 < gidx))).astype(jnp.int32)
    gmask = rank < TOPK_GROUPS
    neg_inf = jnp.float32(-jnp.inf)
    masked = [jnp.where(gmask, b[k], neg_inf) for k in range(per)]
    eidx = [gidx * per + k for k in range(per)]
    sel = [jnp.zeros((ng, t), jnp.bool_) for _ in range(per)]
    for _ in range(TOP_K):
        m = jnp.max(jnp.maximum(jnp.maximum(masked[0], masked[1]), jnp.maximum(masked[2], masked[3])),
                    axis=0, keepdims=True)
        cand = [jnp.where(masked[k] == m, eidx[k], N_EXPERTS) for k in range(per)]
        imin = jnp.min(jnp.minimum(jnp.minimum(cand[0], cand[1]), jnp.minimum(cand[2], cand[3])),
                       axis=0, keepdims=True)
        for k in range(per):
            hit = eidx[k] == imin
            sel[k] = sel[k] | hit
            masked[k] = jnp.where(hit, neg_inf, masked[k])
    w = [jnp.where(sel[k], a[k], 0.0) for k in range(per)]
    denom = jnp.sum(w[0] + w[1] + w[2] + w[3], axis=0, keepdims=True)
    gates = [w[k] / denom * ROUTED_SCALE for k in range(per)]
    return jnp.concatenate(gates + [jnp.zeros((LANES - N_EXPERTS, t), F32)], axis=0)


def _moe_kernel(x_ref, shift_ref, scale_ref, gate_ref, rw_ref, rb_ref, w13_ref, w2_ref, sw13_ref, sw2_ref,
                lng_ref, lnb_ref, o_ref, h_ref, acc_ref, g_ref):
    j = pl.program_id(1)
    ff = EXPERT_FF

    @pl.when(j == 0)
    def _():
        h = (x_ref[...] * (1.0 + scale_ref[...]) + shift_ref[...]).astype(BF16)
        h_ref[...] = h
        scores = _sigmoid(jnp.dot(h, rw_ref[...], preferred_element_type=F32))
        g_ref[...] = _route(scores.T, rb_ref).T
        ab = jnp.dot(h, sw13_ref[...], preferred_element_type=F32)
        act = _silu(ab[:, :ff]) * ab[:, ff:]
        acc_ref[...] = jnp.dot(act.astype(BF16), sw2_ref[...], preferred_element_type=F32)

    h = h_ref[...]
    ab = jnp.dot(h, w13_ref[...], preferred_element_type=F32)
    lane = lax.broadcasted_iota(jnp.int32, g_ref.shape, 1)
    gcol = jnp.sum(jnp.where(lane == j, g_ref[...], 0.0), axis=1, keepdims=True)
    act = _silu(ab[:, :ff]) * ab[:, ff:] * gcol
    acc_ref[...] += jnp.dot(act.astype(BF16), w2_ref[...], preferred_element_type=F32)

    @pl.when(j == pl.num_programs(1) - 1)
    def _():
        o_ref[...] = _layer_norm(DN_ALPHA * x_ref[...] + (1.0 + gate_ref[...]) * acc_ref[...],
                                 lng_ref[...], lnb_ref[...])


def moe(x2, mod, rw, rb, w13, w2, sw13, sw2, lng, lnb, seq, tile=1024):
    n, d = x2.shape
    tpb = seq // tile
    ne = w13.shape[0]
    per = N_EXPERTS // N_EXPERT_GROUPS
    emap = lambda j: (j % N_EXPERT_GROUPS) * per + j // N_EXPERT_GROUPS
    row = pl.BlockSpec((tile, d), lambda i, j: (i, 0))
    modspec = lambda k: pl.BlockSpec((None, 1, d), lambda i, j: (i // tpb, 0, k))
    return pl.pallas_call(
        _moe_kernel,
        grid=(n // tile, ne),
        in_specs=[row, modspec(0), modspec(1), modspec(2),
                  _const_spec((d, LANES)), _const_spec((N_EXPERTS, 1)),
                  pl.BlockSpec((None, d, 2 * EXPERT_FF), lambda i, j: (emap(j), 0, 0)),
                  pl.BlockSpec((None, EXPERT_FF, d), lambda i, j: (emap(j), 0, 0)),
                  _const_spec((d, 2 * EXPERT_FF)), _const_spec((EXPERT_FF, d)),
                  _const_spec((1, d)), _const_spec((1, d))],
        out_specs=row,
        out_shape=jax.ShapeDtypeStruct((n, d), F32),
        scratch_shapes=[pltpu.VMEM((tile, d), BF16), pltpu.VMEM((tile, d), F32), pltpu.VMEM((tile, LANES), F32)],
        compiler_params=_cparams("parallel", "arbitrary"),
        name="moe",
    )(x2, mod, mod, mod, rw, rb, w13, w2, sw13, sw2, lng, lnb)


def _pack_w_in(w):
    d = w.shape[0]
    cuts = np.cumsum([0, 256, 256, 256, 64, 64, 256, 64, 4, 256, 256, 512, 4])
    gm_u, gm_v, q, k, v, qi, ki, wi, s5, z, xbc, dt = [w[:, cuts[i]:cuts[i + 1]] for i in range(12)]
    pad = jnp.zeros((d, 256 - (64 * 3 + 8)), w.dtype)
    return jnp.concatenate([gm_u, gm_v, q, qi, k, v, ki, wi, dt, pad, s5, z, xbc], axis=1).astype(BF16)


def _rope_tables(positions):
    rot = 2 * ROPE_HALF
    inv = ROPE_THETA ** (-jnp.arange(0, rot, 2, dtype=F32) / rot)
    ang = positions.astype(F32).reshape(-1, 1) * inv
    cos, sin = jnp.cos(ang), jnp.sin(ang)
    n = ang.shape[0]
    rest = ATT_HEAD_DIM - rot
    one = jnp.ones((n, rest), F32)
    zero = jnp.zeros((n, rest), F32)
    z8 = jnp.zeros((n, ROPE_HALF), F32)
    rc = jnp.tile(jnp.concatenate([cos, cos, one], axis=1), (1, 2))
    ra = jnp.tile(jnp.concatenate([-sin, z8, zero], axis=1), (1, 2))
    rb = jnp.tile(jnp.concatenate([z8, sin, zero], axis=1), (1, 2))
    return rc, ra, rb


def _lane_pad(vec, offset):
    out = jnp.zeros((1, LANES), F32)
    return out.at[0, offset:offset + vec.shape[0]].set(vec.astype(F32))


def kernel(x, c, positions, mod1_w, mod1_b, w_in, gm_ln_g, gm_ln_b, gm_w, gm_b, s5_lam_re, s5_lam_im, s5_log_step, s5_b_re, s5_b_im, s5_c_re, s5_c_im, s5_d, s5_glu_w, s5_glu_b, ssd_conv_w, ssd_conv_b, ssd_dt_bias, ssd_a_log, ssd_d, ssd_norm_g, w_branch, w_gate, w_out, ln1_g, ln1_b, mod2_w, mod2_b, router_w, router_bias, exp_w1, exp_w3, exp_w2, sh_w1, sh_w3, sh_w2, ln2_g, ln2_b):
    bsz, seq, d = x.shape
    n = bsz * seq
    depth = w_in.shape[0]
    x2 = x.reshape(n, d)

    c_pad = jnp.zeros((SUBLANES, d), F32).at[:bsz].set(c)
    mods1 = adaln_all(c_pad, mod1_w, mod1_b[:, None, :])[:, :, None, :]
    mods2 = adaln_all(c_pad, mod2_w, mod2_b[:, None, :])[:, :, None, :]
    rc, ra, rb = _rope_tables(positions)

    tri = jnp.tril(jnp.ones((GM_CHUNK, GM_CHUNK), F32))
    per = N_EXPERTS // N_EXPERT_GROUPS
    eperm = np.array([(r % N_EXPERT_GROUPS) * per + r // N_EXPERT_GROUPS for r in range(N_EXPERTS)])
    kc = DSA_KEY_CHUNK
    L = S5_CHUNK
    n_chunks = seq // L

    for l in range(depth):
        ya, q, qi, kvk, aux, s5u, z, xbc = inproj(
            x2, mods1[l], _pack_w_in(w_in[l]), rc, ra, rb,
            gm_ln_g[l][None], gm_ln_b[l][None], (gm_w[l] * tri).astype(BF16),
            jnp.repeat(gm_b[l].T, BRANCH_W // GM_GROUPS, axis=1), seq, 512)

        vt = kvk[:, KV_V:KV_V + ATT_HEAD_DIM].reshape(bsz, seq // kc, kc, ATT_HEAD_DIM).transpose(0, 1, 3, 2)
        yb = dsa(qi, q, aux, kvk, vt, bsz, seq, kc=kc)

        w5, e5, al5 = s5_tables(s5_lam_re[l], s5_lam_im[l], s5_log_step[l], s5_b_re[l], s5_b_im[l],
                                s5_c_re[l], s5_c_im[l])
        u_g = (s5u.astype(BF16).reshape(bsz, n_chunks, L, S5_GROUPS, S5_GROUP)
               .transpose(3, 1, 0, 2, 4).reshape(S5_GROUPS, n_chunks * bsz, L * S5_GROUP))
        y_g = s5_scan(u_g, w5, e5, al5, n_chunks, bsz)
        yc = (y_g.reshape(S5_GROUPS, n_chunks, bsz, L, S5_GROUP).transpose(2, 1, 3, 0, 4).reshape(n, BRANCH_W))

        yd = ssd(z, xbc, aux, ssd_conv_w[l], ssd_conv_b[l][None],
                 _lane_pad(ssd_dt_bias[l], KV_DT - 128), _lane_pad(-jnp.exp(ssd_a_log[l]), KV_DT - 128),
                 jnp.repeat(ssd_d[l], SSD_HEAD_DIM)[None], ssd_norm_g[l][None], bsz, seq)

        x2 = merge(x2, mods1[l], ya, yb, yc, s5u, yd, s5_d[l][None], s5_glu_w[l].astype(BF16), s5_glu_b[l][None],
                   w_gate[l].astype(BF16), w_branch[l].astype(BF16), w_out[l].astype(BF16),
                   ln1_g[l][None], ln1_b[l][None], seq)

        rw = jnp.zeros((d, LANES), F32).at[:, :N_EXPERTS].set(router_w[l][:, eperm]).astype(BF16)
        x2 = moe(x2, mods2[l], rw, router_bias[l][eperm][:, None],
                 jnp.concatenate([exp_w1[l], exp_w3[l]], axis=2).astype(BF16), exp_w2[l].astype(BF16),
                 jnp.concatenate([sh_w1[l], sh_w3[l]], axis=1).astype(BF16), sh_w2[l].astype(BF16),
                 ln2_g[l][None], ln2_b[l][None], seq)

    return x2.reshape(bsz, seq, d)
```

```python
import functools
import math

import jax
import jax.numpy as jnp
import numpy as np
from jax import lax
from jax.experimental import pallas as pl
from jax.experimental.pallas import tpu as pltpu

F32 = jnp.float32
BF16 = jnp.bfloat16

D_MODEL = 1024
DEPTH = 2
BRANCH_W = 256
GM_CHUNK = 128
GM_GROUPS = 4
ATT_HEAD_DIM = 64
ATT_HEADS = 4
IDX_HEADS = 4
IDX_DIM = 64
TOPK_MAX = 256
ROPE_THETA = 500000.0
ROPE_HALF = 8
S5_GROUP = 16
S5_GROUPS = 16
S5_STATE = 64
S5_CHUNK = 32
S5_CHUNKS_PER_ITER = 8
SSD_HEAD_DIM = 64
SSD_HEADS = 4
SSD_GROUPS = 2
SSD_STATE = 64
SSD_CONV = 4
SSD_CHUNK = 128
SSD_XBC = 512
N_EXPERTS = 32
TOP_K = 8
N_EXPERT_GROUPS = 8
TOPK_GROUPS = 4
EXPERT_FF = 256
ROUTED_SCALE = 2.5
DN_ALPHA = (2 * DEPTH) ** 0.25
LN_EPS = 1e-5
RMS_EPS = 1e-6

LANES = 128
SUBLANES = 8
VMEM_LIMIT_BYTES = 56 * 1024 * 1024

C_GMU, C_GMV, C_Q, C_QI, C_KV, C_S5, C_Z, C_XBC = 0, 256, 512, 768, 1024, 1280, 1536, 1792
D_IN_PACKED = 2304
KV_K, KV_V, KV_KI, KV_WI, KV_DT = 0, 64, 128, 192, 196

NEG_BIG = -1e30
INT_MIN = -2 ** 31


def _cparams(*sem):
    return pltpu.CompilerParams(dimension_semantics=sem, vmem_limit_bytes=VMEM_LIMIT_BYTES)


def _const_spec(shape):
    nd = len(shape)
    return pl.BlockSpec(shape, lambda *_: (0,) * nd, pipeline_mode=pl.Buffered(1))


def _layer_norm(v, g, b):
    mu = jnp.mean(v, axis=-1, keepdims=True)
    d = v - mu
    var = jnp.mean(d * d, axis=-1, keepdims=True)
    return d * lax.rsqrt(var + LN_EPS) * g + b


def _sigmoid(v):
    return 1.0 / (1.0 + jnp.exp(-v))


def _silu(v):
    return v * _sigmoid(v)


def _adaln_kernel(c_ref, w_ref, b_ref, o_ref):
    o_ref[...] = jnp.dot(c_ref[...], w_ref[...], preferred_element_type=F32,
                         precision=lax.Precision.HIGHEST) + b_ref[...]


def adaln_all(c_pad, w_all, b_all):
    m, d, d3 = w_all.shape
    bn = 512
    return pl.pallas_call(
        _adaln_kernel,
        grid=(m, d3 // bn),
        in_specs=[pl.BlockSpec((SUBLANES, d), lambda i, j: (0, 0)),
                  pl.BlockSpec((None, d, bn), lambda i, j: (i, 0, j)),
                  pl.BlockSpec((None, 1, bn), lambda i, j: (i, 0, j))],
        out_specs=pl.BlockSpec((None, SUBLANES, bn), lambda i, j: (i, 0, j)),
        out_shape=jax.ShapeDtypeStruct((m, SUBLANES, d3), F32),
        compiler_params=_cparams("parallel", "parallel"),
        name="adaln",
    )(c_pad, w_all, b_all)


def _rope128(t, c, a, b):
    return t * c + pltpu.roll(t, LANES - ROPE_HALF, axis=1) * a + pltpu.roll(t, ROPE_HALF, axis=1) * b


def _inproj_kernel(x_ref, shift_ref, scale_ref, w_ref, rc_ref, ra_ref, rb_ref,
                   lng_ref, lnb_ref, gmw_ref, gmb_ref,
                   ya_ref, q_ref, qi_ref, kvk_ref, aux_ref, s5u_ref, z_ref, xbc_ref):
    t = x_ref.shape[0]
    h = (x_ref[...] * (1.0 + scale_ref[...]) + shift_ref[...]).astype(BF16)
    rc, ra, rb = rc_ref[...], ra_ref[...], rb_ref[...]

    def proj(c0, width):
        return jnp.dot(h, w_ref[:, c0:c0 + width], preferred_element_type=F32)

    for c0, o_ref, mul in ((C_Q, q_ref, ATT_HEAD_DIM ** -0.5 * math.log2(math.e)), (C_QI, qi_ref, 1.0)):
        p = proj(c0, 256)
        o_ref[:, 0:128] = (_rope128(p[:, 0:128], rc, ra, rb) * mul).astype(BF16)
        o_ref[:, 128:256] = (_rope128(p[:, 128:256], rc, ra, rb) * mul).astype(BF16)

    p = proj(C_KV, 256)
    lane = lax.broadcasted_iota(jnp.int32, (t, LANES), 1)
    first = lane < ATT_HEAD_DIM
    rc1 = jnp.where(first, rc, 1.0)
    ra1 = jnp.where(first, ra, 0.0)
    rb1 = jnp.where(first, rb, 0.0)
    kv = _rope128(p[:, 0:128], rc1, ra1, rb1)
    kiw = _rope128(p[:, 128:256], rc1, ra1, rb1)
    kvk_ref[:, 0:128] = kv.astype(BF16)
    kvk_ref[:, 128:256] = kiw.astype(BF16)
    aux_ref[...] = kiw

    s5u_ref[...] = proj(C_S5, 256)
    z_ref[...] = proj(C_Z, 256)
    xbc_ref[...] = proj(C_XBC, 512)

    u = jax.nn.gelu(proj(C_GMU, 256))
    v = _layer_norm(jax.nn.gelu(proj(C_GMV, 256)), lng_ref[...], lnb_ref[...]).astype(BF16)
    gw = GM_CHUNK
    cw = BRANCH_W // GM_GROUPS
    for ci in range(t // gw):
        rows = slice(ci * gw, (ci + 1) * gw)
        for g in range(GM_GROUPS):
            cols = slice(g * cw, (g + 1) * cw)
            sv = jnp.dot(gmw_ref[g], v[rows, cols], preferred_element_type=F32) + gmb_ref[:, cols]
            ya_ref[rows, cols] = (u[rows, cols] * sv).astype(BF16)


def inproj(x2, mod, w_in_p, rc, ra, rb, lng, lnb, gmw, gmb, seq, tile):
    n, d = x2.shape
    tpb = seq // tile
    row = lambda width: pl.BlockSpec((tile, width), lambda i: (i, 0))
    outs = [(256, BF16), (256, BF16), (256, BF16), (256, BF16), (128, F32), (256, F32), (256, F32), (512, F32)]
    return pl.pallas_call(
        _inproj_kernel,
        grid=(n // tile,),
        in_specs=[row(d),
                  pl.BlockSpec((None, 1, d), lambda i: (i // tpb, 0, 0)),
                  pl.BlockSpec((None, 1, d), lambda i: (i // tpb, 0, 1)),
                  _const_spec((d, D_IN_PACKED)),
                  row(LANES), row(LANES), row(LANES),
                  _const_spec((1, 256)), _const_spec((1, 256)),
                  _const_spec((GM_GROUPS, GM_CHUNK, GM_CHUNK)), _const_spec((GM_CHUNK, 256))],
        out_specs=[row(w) for w, _ in outs],
        out_shape=[jax.ShapeDtypeStruct((n, w), dt) for w, dt in outs],
        compiler_params=_cparams("parallel"),
        name="inproj",
    )(x2, mod, mod, w_in_p, rc, ra, rb, lng, lnb, gmw, gmb)


def _dsa_kernel(qi_ref, q_ref, aux_ref, kvk_ref, vt_ref, o_ref, key_ref, planes_ref, mask_ref, lg0_ref, lg1_ref,
                *, tq, kc, n_sel):
    j = pl.program_id(1)
    nh = ATT_HEADS
    n_it = ((j + 1) * tq + 2 * kc - 1) // (2 * kc)
    n_rows = n_it * 2 * kc
    nt = (((1,), (1,)), ((), ()))

    qih = jnp.concatenate([qi_ref[:, h * IDX_DIM:(h + 1) * IDX_DIM] for h in range(IDX_HEADS)], axis=0)
    qh = jnp.concatenate([q_ref[:, h * ATT_HEAD_DIM:(h + 1) * ATT_HEAD_DIM] for h in range(nh)], axis=0)
    aux_t = aux_ref[...].T
    w_scale = IDX_HEADS ** -0.5 * IDX_DIM ** -0.5
    wrow = [aux_t[KV_WI - 128 + h:KV_WI - 128 + h + 1, :] * w_scale for h in range(IDX_HEADS)]

    qpos = j * tq + lax.broadcasted_iota(jnp.int32, (kc, tq), 1)
    krel = lax.broadcasted_iota(jnp.int32, (kc, tq), 0)

    last_k0 = kvk_ref.shape[0] - kc

    def logits(col0, q_rows, k0):
        kk = kvk_ref[pl.ds(pl.multiple_of(jnp.minimum(k0, last_k0), kc), kc), col0:col0 + IDX_DIM]
        return lax.dot_general(kk, q_rows, nt, preferred_element_type=F32)

    def pipelined(col0, q_rows, consume, carry):
        lg0_ref[...] = logits(col0, q_rows, 0)

        def pair(i, carry):
            k0 = pl.multiple_of(i * 2 * kc, 2 * kc)
            lg1_ref[...] = logits(col0, q_rows, k0 + kc)
            carry = consume(lg0_ref, k0, 2 * i, carry)
            lg0_ref[...] = logits(col0, q_rows, k0 + 2 * kc)
            return consume(lg1_ref, k0 + kc, 2 * i + 1, carry)

        return lax.fori_loop(0, n_it, pair, carry)

    def score_chunk(lg_ref, k0, c, carry):
        sc = jnp.zeros((kc, tq), F32)
        for h in range(IDX_HEADS):
            sc = sc + jnp.maximum(lg_ref[:, h * tq:(h + 1) * tq], 0.0) * wrow[h]
        sc = jnp.where(sc == 0.0, 0.0, sc)
        bits = lax.bitcast_convert_type(sc, jnp.int32)
        key = bits ^ ((bits >> 31) & jnp.int32(0x7FFFFFFF))
        key_ref[pl.ds(k0, kc), :] = jnp.where(k0 + krel <= qpos, key, jnp.int32(INT_MIN))
        return carry

    pipelined(KV_KI, qih, score_chunk, 0)

    wb = 32
    sbr = wb * SUBLANES

    def to_planes(sb, carry):
        r0 = pl.multiple_of(sb * sbr, sbr)
        rows = pl.ds(pl.multiple_of(sb * SUBLANES, SUBLANES), SUBLANES)
        for l0 in range(0, tq, LANES):
            cols = slice(l0, l0 + LANES)
            a = [key_ref[pl.ds(r0 + SUBLANES * w, SUBLANES), cols] ^ jnp.int32(INT_MIN) for w in range(wb)]
            half, m = wb // 2, 0x0000FFFF
            while half:
                k = 0
                while k < wb:
                    t = (a[k] ^ lax.shift_right_logical(a[k + half], half)) & jnp.int32(np.uint32(m).astype(np.int32))
                    a[k] = a[k] ^ t
                    a[k + half] = a[k + half] ^ (t << half)
                    k = (k + half + 1) & ~half
                half >>= 1
                m ^= (m << half) & 0xFFFFFFFF
            for p in range(wb):
                planes_ref[p, rows, cols] = a[p]
        return carry

    lax.fori_loop(0, n_rows // sbr, to_planes, 0)

    pr = planes_ref.shape[1]
    prow = lax.broadcasted_iota(jnp.int32, (pr, tq), 0)
    live0 = jnp.where(prow < n_rows // wb, jnp.int32(-1), jnp.int32(0))

    def radix_step(live, remaining, plane):
        ones = jnp.sum(lax.population_count(live & plane), axis=0, keepdims=True)
        take = ones >= remaining
        remaining = jnp.where(take, remaining, remaining - ones)
        live = live & (plane ^ jnp.where(take, 0, -1))
        return live, remaining, take

    def score_pass(p, carry):
        live, remaining, thr_u = carry
        live, remaining, take = radix_step(live, remaining, planes_ref[p])
        thr_u = thr_u | jnp.where(take, lax.shift_right_logical(jnp.int32(INT_MIN), p), 0)
        return live, remaining, thr_u

    live, remaining, thr_u = lax.fori_loop(
        0, wb, score_pass, (live0, jnp.full((1, tq), n_sel, jnp.int32), jnp.zeros((1, tq), jnp.int32)))
    thr = thr_u ^ jnp.int32(INT_MIN)

    n_tiles = pr // SUBLANES
    tile_inv = (n_tiles - 1) - (prow >> 3)
    sub_inv = (SUBLANES - 1) - (prow & (SUBLANES - 1))
    lane_bit = [0xAAAAAAAA, 0xCCCCCCCC, 0xF0F0F0F0, 0xFF00FF00, 0xFFFF0000]
    inv_planes = ([(8 + t, -((tile_inv >> t) & 1)) for t in range(max(n_tiles - 1, 1).bit_length())]
                  + [(3 + t, jnp.full((pr, tq), np.uint32(lane_bit[t]).astype(np.int32), jnp.int32)) for t in range(5)]
                  + [(t, -((sub_inv >> t) & 1)) for t in range(3)])
    inv_cut = jnp.zeros((1, tq), jnp.int32)
    for bit, plane in sorted(inv_planes, key=lambda e: -e[0]):
        live, remaining, take = radix_step(live, remaining, plane)
        inv_cut = inv_cut | jnp.where(take, 1 << bit, 0)
    row_cut = (pr * wb - 1) - inv_cut

    tb = 256
    rrel = lax.broadcasted_iota(jnp.int32, (tb, tq), 0)

    def flag_rows(i, carry):
        r0 = pl.multiple_of(i * tb, tb)
        blk = key_ref[pl.ds(r0, tb), :]
        keep = ((blk > thr) | ((blk == thr) & (r0 + rrel <= row_cut))) & (blk > INT_MIN)
        mask_ref[pl.ds(r0, tb), :] = jnp.where(keep, 0.0, NEG_BIG)
        return carry

    lax.fori_loop(0, n_rows // tb, flag_rows, 0)

    def att_chunk(lg_ref, k0, c, carry):
        m, l, acc = carry
        madd = mask_ref[pl.ds(k0, kc), :]
        m_parts, p_parts = [], []
        for h in range(nh):
            hs = slice(h * tq, (h + 1) * tq)
            lgh = lg_ref[:, hs] + madd
            mh = jnp.maximum(m[:, hs], jnp.max(lgh, axis=0, keepdims=True))
            m_parts.append(mh)
            p_parts.append(jnp.exp2(lgh - mh))
        m_new = jnp.concatenate(m_parts, axis=1)
        p = jnp.concatenate(p_parts, axis=1)
        alpha = jnp.exp2(m - m_new)
        l = l * alpha + jnp.sum(p, axis=0, keepdims=True)
        acc = acc * alpha + jnp.dot(vt_ref[c], p.astype(BF16), preferred_element_type=F32)
        return m_new, l, acc

    m0 = jnp.full((1, nh * tq), NEG_BIG, F32)
    l0 = jnp.zeros((1, nh * tq), F32)
    a0 = jnp.zeros((ATT_HEAD_DIM, nh * tq), F32)
    _, l, acc = pipelined(KV_K, qh, att_chunk, (m0, l0, a0))
    o = acc / l
    o = jnp.concatenate([o, jnp.zeros_like(o)], axis=0)
    for h in range(nh):
        o_ref[:, h * ATT_HEAD_DIM:(h + 1) * ATT_HEAD_DIM] = (
            o[:, h * tq:(h + 1) * tq].T[:, :ATT_HEAD_DIM].astype(o_ref.dtype))


DSA_KEY_CHUNK = 256


DSA_QUERY_TILE = 256


def dsa(qi, q, aux, kvk, vt, batch, seq, tq=DSA_QUERY_TILE, kc=DSA_KEY_CHUNK):
    n = qi.shape[0]
    nq = seq // tq
    n_sel = min(TOPK_MAX, seq // 4)
    blk = lambda w: pl.BlockSpec((tq, w), lambda b, j: (b * nq + j, 0))
    return pl.pallas_call(
        functools.partial(_dsa_kernel, tq=tq, kc=kc, n_sel=n_sel),
        grid=(batch, nq),
        in_specs=[blk(256), blk(256), blk(LANES),
                  pl.BlockSpec((seq, 256), lambda b, j: (b, 0), pipeline_mode=pl.Buffered(1)),
                  pl.BlockSpec((None, seq // kc, ATT_HEAD_DIM, kc), lambda b, j: (b, 0, 0, 0),
                               pipeline_mode=pl.Buffered(1))],
        out_specs=blk(256),
        out_shape=jax.ShapeDtypeStruct((n, 256), BF16),
        scratch_shapes=[pltpu.VMEM((seq, tq), jnp.int32),
                        pltpu.VMEM((32, seq // 32, tq), jnp.int32),
                        pltpu.VMEM((seq, tq), F32),
                        pltpu.VMEM((kc, ATT_HEADS * tq), F32),
                        pltpu.VMEM((kc, ATT_HEADS * tq), F32)],
        compiler_params=_cparams("parallel", "arbitrary"),
        name="dsa",
    )(qi, q, aux, kvk, vt)


def s5_tables(lam_re, lam_im, log_step, b_re, b_im, c_re, c_im):
    hp = lax.Precision.HIGHEST
    L = S5_CHUNK
    g_, p_ = lam_re.shape
    step = jnp.exp(log_step)[:, None]
    d = jnp.arange(L + 1, dtype=F32)[None, :, None]
    mag = jnp.exp(d * (lam_re * step)[:, None, :])
    ang = d * (lam_im * step)[:, None, :]
    pw_re, pw_im = mag * jnp.cos(ang), mag * jnp.sin(ang)
    ab_re, ab_im = pw_re[:, 1], pw_im[:, 1]
    den = lam_re * lam_re + lam_im * lam_im
    nr = ab_re - 1.0
    cr = (nr * lam_re + ab_im * lam_im) / den
    ci = (ab_im * lam_re - nr * lam_im) / den
    bb_re = cr[..., None] * b_re - ci[..., None] * b_im
    bb_im = cr[..., None] * b_im + ci[..., None] * b_re
    cp_re = c_re[:, None] * pw_re[:, :, None, :] - c_im[:, None] * pw_im[:, :, None, :]
    cp_im = c_re[:, None] * pw_im[:, :, None, :] + c_im[:, None] * pw_re[:, :, None, :]
    kern = (jnp.einsum('gdop,gpi->gdoi', cp_re, bb_re, precision=hp)
            - jnp.einsum('gdop,gpi->gdoi', cp_im, bb_im, precision=hp))
    s_i = jnp.arange(L)[:, None]
    t_i = jnp.arange(L)[None, :]
    lag_sel = (t_i - s_i == jnp.arange(L)[:, None, None]).astype(F32)
    toe = jnp.einsum('gdoi,dst->gsito', kern[:, :L], lag_sel, precision=hp)
    toe = toe.reshape(g_, L * S5_GROUP, L * S5_GROUP)
    pw_rev_re, pw_rev_im = pw_re[:, L - 1::-1][:, :L], pw_im[:, L - 1::-1][:, :L]
    gm_re = pw_rev_re[:, :, None, :] * bb_re.transpose(0, 2, 1)[:, None] - pw_rev_im[:, :, None, :] * bb_im.transpose(0, 2, 1)[:, None]
    gm_im = pw_rev_re[:, :, None, :] * bb_im.transpose(0, 2, 1)[:, None] + pw_rev_im[:, :, None, :] * bb_re.transpose(0, 2, 1)[:, None]
    gm_re = gm_re.reshape(g_, L * S5_GROUP, p_)
    gm_im = gm_im.reshape(g_, L * S5_GROUP, p_)
    w = jnp.concatenate([toe, gm_re, gm_im], axis=2).astype(BF16)
    e_re = cp_re[:, 1:].transpose(0, 3, 1, 2).reshape(g_, p_, L * S5_GROUP)
    e_im = -cp_im[:, 1:].transpose(0, 3, 1, 2).reshape(g_, p_, L * S5_GROUP)
    e = jnp.concatenate([e_re, e_im], axis=1).astype(BF16)
    al = jnp.stack([pw_re[:, L], pw_im[:, L]], axis=1)
    return w, e, al


def _s5_kernel(u_ref, w_ref, e_ref, al_ref, y_ref, ys_ref, hp_ref, *, n_chunks, nb):
    lw = u_ref.shape[1]
    p_ = S5_STATE
    ys_ref[...] = jnp.dot(u_ref[...], w_ref[...], preferred_element_type=F32)
    ar = al_ref[0:1, :]
    ai = al_ref[1:2, :]

    cpi = S5_CHUNKS_PER_ITER
    rows_per_iter = cpi * nb
    assert rows_per_iter % SUBLANES == 0 and n_chunks % cpi == 0

    def step(it, carry):
        hr, hi = carry
        rows = pl.ds(pl.multiple_of(it * rows_per_iter, rows_per_iter), rows_per_iter)
        st = ys_ref[rows, lw:lw + 2 * p_]
        hrs, his = [], []
        for k in range(cpi):
            hrs.append(hr)
            his.append(hi)
            sr, si = st[k * nb:(k + 1) * nb, 0:p_], st[k * nb:(k + 1) * nb, p_:2 * p_]
            hr, hi = ar * hr - ai * hi + sr, ar * hi + ai * hr + si
        hp_ref[rows, 0:p_] = jnp.concatenate(hrs, axis=0)
        hp_ref[rows, p_:2 * p_] = jnp.concatenate(his, axis=0)
        return hr, hi

    z = jnp.zeros((nb, p_), F32)
    lax.fori_loop(0, n_chunks // cpi, step, (z, z))
    y_ref[...] = (ys_ref[:, 0:lw] + jnp.dot(hp_ref[...].astype(BF16), e_ref[...], preferred_element_type=F32)
                  ).astype(y_ref.dtype)


def s5_scan(u_g, w, e, al, n_chunks, nb):
    g_, r, lw = u_g.shape
    return pl.pallas_call(
        functools.partial(_s5_kernel, n_chunks=n_chunks, nb=nb),
        grid=(g_,),
        in_specs=[pl.BlockSpec((None, r, lw), lambda g: (g, 0, 0)),
                  pl.BlockSpec((None, lw, lw + 2 * S5_STATE), lambda g: (g, 0, 0)),
                  pl.BlockSpec((None, 2 * S5_STATE, lw), lambda g: (g, 0, 0)),
                  pl.BlockSpec((None, 2, S5_STATE), lambda g: (g, 0, 0))],
        out_specs=pl.BlockSpec((None, r, lw), lambda g: (g, 0, 0)),
        out_shape=jax.ShapeDtypeStruct((g_, r, lw), BF16),
        scratch_shapes=[pltpu.VMEM((r, lw + 2 * S5_STATE), F32), pltpu.VMEM((r, 2 * S5_STATE), F32)],
        compiler_params=_cparams("parallel"),
        name="s5",
    )(u_g, w, e, al)


def _ssd_kernel(z_ref, xbc_ref, aux_ref, cw_ref, cb_ref, dtb_ref, aneg_ref, dsk_ref, ng_ref,
                y_ref, state_ref, halo_ref, *, tile):
    hp = lax.Precision.HIGHEST
    ck = SSD_CHUNK
    hd = SSD_HEAD_DIM

    @pl.when(pl.program_id(1) == 0)
    def _():
        state_ref[...] = jnp.zeros_like(state_ref)
        halo_ref[...] = jnp.zeros_like(halo_ref)

    xbc = xbc_ref[...]
    ext = jnp.concatenate([halo_ref[...], xbc], axis=0)
    halo_ref[...] = xbc[tile - SUBLANES:tile, :]
    conv = cb_ref[...] + sum(cw_ref[jj:jj + 1, :] * ext[SUBLANES - (SSD_CONV - 1) + jj:SUBLANES - (SSD_CONV - 1) + jj + tile, :]
                             for jj in range(SSD_CONV))
    act = _silu(conv)
    x = act[:, 0:BRANCH_W]
    bm = act[:, BRANCH_W:BRANCH_W + SSD_GROUPS * SSD_STATE]
    cm = act[:, BRANCH_W + SSD_GROUPS * SSD_STATE:]
    dt_all = jax.nn.softplus(aux_ref[...] + dtb_ref[...])
    adt_all = dt_all * aneg_ref[...]

    r_i = lax.broadcasted_iota(jnp.int32, (ck, ck), 0)
    c_i = lax.broadcasted_iota(jnp.int32, (ck, ck), 1)
    lower = c_i <= r_i
    tri = lower.astype(F32)
    zs = _silu(z_ref[...])
    ng = ng_ref[...]
    dl = KV_DT - 128

    for ci in range(tile // ck):
        rows = slice(ci * ck, (ci + 1) * ck)
        a_cum = jnp.dot(tri, adt_all[rows], preferred_element_type=F32, precision=hp)
        a_cum_t = a_cum.T
        bm_t = bm[rows].T
        cmb = cm[rows].astype(BF16)
        bmb = bm[rows].astype(BF16)
        gmat = [lax.dot_general(cmb[:, g * SSD_STATE:(g + 1) * SSD_STATE], bmb[:, g * SSD_STATE:(g + 1) * SSD_STATE],
                                (((1,), (1,)), ((), ())), preferred_element_type=F32) for g in range(SSD_GROUPS)]
        ys = []
        for h in range(SSD_HEADS):
            g = h // (SSD_HEADS // SSD_GROUPS)
            acol = a_cum[:, dl + h:dl + h + 1]
            arow = a_cum_t[dl + h:dl + h + 1, :]
            dtc = dt_all[rows, dl + h:dl + h + 1]
            xh = x[rows, h * hd:(h + 1) * hd]
            xdt = xh * dtc
            lmat = jnp.exp(jnp.where(lower, acol - arow, NEG_BIG))
            y_diag = jnp.dot((gmat[g] * lmat).astype(BF16), xdt.astype(BF16), preferred_element_type=F32)
            s_prev = state_ref[h]
            y_off = jnp.exp(acol) * jnp.dot(cmb[:, g * SSD_STATE:(g + 1) * SSD_STATE], s_prev.astype(BF16),
                                            preferred_element_type=F32)
            a_last = a_cum[ck - 1:ck, dl + h:dl + h + 1]
            decay = jnp.exp(a_last - acol)
            s_new = jnp.dot(bm_t[g * SSD_STATE:(g + 1) * SSD_STATE, :].astype(BF16), (xdt * decay).astype(BF16),
                            preferred_element_type=F32)
            state_ref[h] = jnp.exp(a_last) * s_prev + s_new
            ys.append(y_diag + y_off + dsk_ref[:, h * hd:(h + 1) * hd] * xh)
        y = jnp.concatenate(ys, axis=1) * zs[rows]
        gw = BRANCH_W // SSD_GROUPS
        outs = []
        for g in range(SSD_GROUPS):
            yg = y[:, g * gw:(g + 1) * gw]
            outs.append(yg * lax.rsqrt(jnp.mean(yg * yg, axis=-1, keepdims=True) + RMS_EPS))
        y_ref[rows, :] = (jnp.concatenate(outs, axis=1) * ng).astype(y_ref.dtype)


def ssd(z, xbc, aux, cw, cb, dtb, aneg, dsk, ng, batch, seq, tile=512):
    n = z.shape[0]
    nt = seq // tile
    row = lambda w: pl.BlockSpec((tile, w), lambda b, i: (b * nt + i, 0))
    return pl.pallas_call(
        functools.partial(_ssd_kernel, tile=tile),
        grid=(batch, nt),
        in_specs=[row(256), row(512), row(LANES),
                  _const_spec((SSD_CONV, SSD_XBC)), _const_spec((1, SSD_XBC)), _const_spec((1, LANES)),
                  _const_spec((1, LANES)), _const_spec((1, 256)), _const_spec((1, 256))],
        out_specs=row(256),
        out_shape=jax.ShapeDtypeStruct((n, 256), BF16),
        scratch_shapes=[pltpu.VMEM((SSD_HEADS, SSD_STATE, SSD_HEAD_DIM), F32), pltpu.VMEM((SUBLANES, SSD_XBC), F32)],
        compiler_params=_cparams("parallel", "arbitrary"),
        name="ssd",
    )(z, xbc, aux, cw, cb, dtb, aneg, dsk, ng)


def _merge_kernel(x_ref, shift_ref, scale_ref, gate_ref, ya_ref, yb_ref, yc_ref, s5u_ref, yd_ref,
                  s5d_ref, gluw_ref, glub_ref, wg_ref, wb_ref, wo_ref, lng_ref, lnb_ref, o_ref):
    x = x_ref[...]
    h = (x * (1.0 + scale_ref[...]) + shift_ref[...]).astype(BF16)
    yc = jax.nn.gelu(yc_ref[...] + s5d_ref[...] * s5u_ref[...])
    yc = yc * _sigmoid(jnp.dot(yc.astype(BF16), gluw_ref[...], preferred_element_type=F32) + glub_ref[...])
    branches = (ya_ref[...], yb_ref[...], yc.astype(BF16), yd_ref[...])
    merged = None
    for i, br in enumerate(branches):
        gate = _sigmoid(jnp.dot(h, wg_ref[i], preferred_element_type=F32))
        term = gate * jnp.dot(br, wb_ref[i], preferred_element_type=F32)
        merged = term if merged is None else merged + term
    mix = jnp.dot(merged.astype(BF16), wo_ref[...], preferred_element_type=F32)
    o_ref[...] = _layer_norm(DN_ALPHA * x + (1.0 + gate_ref[...]) * mix, lng_ref[...], lnb_ref[...])


def merge(x2, mod, ya, yb, yc, s5u, yd, s5d, gluw, glub, wg, wb, wo, lng, lnb, seq, tile=512):
    n, d = x2.shape
    tpb = seq // tile
    row = lambda w: pl.BlockSpec((tile, w), lambda i: (i, 0))
    modspec = lambda k: pl.BlockSpec((None, 1, d), lambda i: (i // tpb, 0, k))
    return pl.pallas_call(
        _merge_kernel,
        grid=(n // tile,),
        in_specs=[row(d), modspec(0), modspec(1), modspec(2),
                  row(256), row(256), row(256), row(256), row(256),
                  _const_spec((1, 256)), _const_spec((256, 256)), _const_spec((1, 256)),
                  _const_spec((4, d, d)), _const_spec((4, 256, d)), _const_spec((d, d)),
                  _const_spec((1, d)), _const_spec((1, d))],
        out_specs=row(d),
        out_shape=jax.ShapeDtypeStruct((n, d), F32),
        compiler_params=_cparams("parallel"),
        name="merge",
    )(x2, mod, mod, mod, ya, yb, yc, s5u, yd, s5d, gluw, glub, wg, wb, wo, lng, lnb)


def _route(scores_t, bias_ref):
    ng = N_EXPERT_GROUPS
    per = N_EXPERTS // ng
    t = scores_t.shape[1]
    a = [scores_t[k * ng:(k + 1) * ng] for k in range(per)]
    b = [a[k] + bias_ref[k * ng:(k + 1) * ng, :] for k in range(per)]
    hi1, lo1 = jnp.maximum(b[0], b[1]), jnp.minimum(b[0], b[1])
    hi2, lo2 = jnp.maximum(b[2], b[3]), jnp.minimum(b[2], b[3])
    gs = jnp.maximum(hi1, hi2) + jnp.maximum(jnp.minimum(hi1, hi2), jnp.maximum(lo1, lo2))
    gidx = lax.broadcasted_iota(jnp.int32, (ng, t), 0)
    rank = jnp.zeros((ng, t), jnp.int32)
    for g2 in range(ng):
        row = gs[g2:g2 + 1, :]
        rank = rank + ((row > gs) | ((row == gs) & (g2 < gidx))).astype(jnp.int32)
    gmask = rank < TOPK_GROUPS
    neg_inf = jnp.float32(-jnp.inf)
    masked = [jnp.where(gmask, b[k], neg_inf) for k in range(per)]
    eidx = [gidx * per + k for k in range(per)]
    sel = [jnp.zeros((ng, t), jnp.bool_) for _ in range(per)]
    for _ in range(TOP_K):
        m = jnp.max(jnp.maximum(jnp.maximum(masked[0], masked[1]), jnp.maximum(masked[2], masked[3])),
                    axis=0, keepdims=True)
        cand = [jnp.where(masked[k] == m, eidx[k], N_EXPERTS) for k in range(per)]
        imin = jnp.min(jnp.minimum(jnp.minimum(cand[0], cand[1]), jnp.minimum(cand[2], cand[3])),
                       axis=0, keepdims=True)
        for k in range(per):
            hit = eidx[k] == imin
            sel[k] = sel[k] | hit
            masked[k] = jnp.where(hit, neg_inf, masked[k])
    w = [jnp.where(sel[k], a[k], 0.0) for k in range(per)]
    denom = jnp.sum(w[0] + w[1] + w[2] + w[3], axis=0, keepdims=True)
    gates = [w[k] / denom * ROUTED_SCALE for k in range(per)]
    return jnp.concatenate(gates + [jnp.zeros((LANES - N_EXPERTS, t), F32)], axis=0)


def _moe_kernel(x_ref, shift_ref, scale_ref, gate_ref, rw_ref, rb_ref, w13a_ref, w2a_ref, w13b_ref, w2b_ref,
                sw13_ref, sw2_ref, lng_ref, lnb_ref, o_ref, h_ref, acc_ref, g_ref):
    j = pl.program_id(1)
    ff = EXPERT_FF

    @pl.when(j == 0)
    def _():
        h = (x_ref[...] * (1.0 + scale_ref[...]) + shift_ref[...]).astype(BF16)
        h_ref[...] = h
        scores = _sigmoid(jnp.dot(h, rw_ref[...], preferred_element_type=F32))
        g_ref[...] = _route(scores.T, rb_ref).T
        ab = jnp.dot(h, sw13_ref[...], preferred_element_type=F32)
        act = _silu(ab[:, :ff]) * ab[:, ff:]
        acc_ref[...] = jnp.dot(act.astype(BF16), sw2_ref[...], preferred_element_type=F32)

    h = h_ref[...]
    lane = lax.broadcasted_iota(jnp.int32, g_ref.shape, 1)
    gates = g_ref[...]
    out = None
    for k, (w13_ref, w2_ref) in enumerate(((w13a_ref, w2a_ref), (w13b_ref, w2b_ref))):
        ab = jnp.dot(h, w13_ref[...], preferred_element_type=F32)
        gcol = jnp.sum(jnp.where(lane == 2 * j + k, gates, 0.0), axis=1, keepdims=True)
        act = _silu(ab[:, :ff]) * ab[:, ff:] * gcol
        y = jnp.dot(act.astype(BF16), w2_ref[...], preferred_element_type=F32)
        out = y if out is None else out + y
    acc_ref[...] += out

    @pl.when(j == pl.num_programs(1) - 1)
    def _():
        o_ref[...] = _layer_norm(DN_ALPHA * x_ref[...] + (1.0 + gate_ref[...]) * acc_ref[...],
                                 lng_ref[...], lnb_ref[...])


def moe(x2, mod, rw, rb, w13, w2, sw13, sw2, lng, lnb, seq, tile=1024):
    n, d = x2.shape
    tpb = seq // tile
    ne = w13.shape[0]
    per = N_EXPERTS // N_EXPERT_GROUPS
    emap = lambda j: (j % N_EXPERT_GROUPS) * per + j // N_EXPERT_GROUPS
    row = pl.BlockSpec((tile, d), lambda i, j: (i, 0))
    modspec = lambda k: pl.BlockSpec((None, 1, d), lambda i, j: (i // tpb, 0, k))
    w13spec = lambda k: pl.BlockSpec((None, d, 2 * EXPERT_FF), lambda i, j: (emap(2 * j + k), 0, 0))
    w2spec = lambda k: pl.BlockSpec((None, EXPERT_FF, d), lambda i, j: (emap(2 * j + k), 0, 0))
    return pl.pallas_call(
        _moe_kernel,
        grid=(n // tile, ne // 2),
        in_specs=[row, modspec(0), modspec(1), modspec(2),
                  _const_spec((d, LANES)), _const_spec((N_EXPERTS, 1)),
                  w13spec(0), w2spec(0), w13spec(1), w2spec(1),
                  _const_spec((d, 2 * EXPERT_FF)), _const_spec((EXPERT_FF, d)),
                  _const_spec((1, d)), _const_spec((1, d))],
        out_specs=row,
        out_shape=jax.ShapeDtypeStruct((n, d), F32),
        scratch_shapes=[pltpu.VMEM((tile, d), BF16), pltpu.VMEM((tile, d), F32), pltpu.VMEM((tile, LANES), F32)],
        compiler_params=_cparams("parallel", "arbitrary"),
        name="moe",
    )(x2, mod, mod, mod, rw, rb, w13, w2, w13, w2, sw13, sw2, lng, lnb)


def _pack_w_in(w):
    d = w.shape[0]
    cuts = np.cumsum([0, 256, 256, 256, 64, 64, 256, 64, 4, 256, 256, 512, 4])
    gm_u, gm_v, q, k, v, qi, ki, wi, s5, z, xbc, dt = [w[:, cuts[i]:cuts[i + 1]] for i in range(12)]
    pad = jnp.zeros((d, 256 - (64 * 3 + 8)), w.dtype)
    return jnp.concatenate([gm_u, gm_v, q, qi, k, v, ki, wi, dt, pad, s5, z, xbc], axis=1).astype(BF16)


def _rope_tables(positions):
    rot = 2 * ROPE_HALF
    inv = ROPE_THETA ** (-jnp.arange(0, rot, 2, dtype=F32) / rot)
    ang = positions.astype(F32).reshape(-1, 1) * inv
    cos, sin = jnp.cos(ang), jnp.sin(ang)
    n = ang.shape[0]
    rest = ATT_HEAD_DIM - rot
    one = jnp.ones((n, rest), F32)
    zero = jnp.zeros((n, rest), F32)
    z8 = jnp.zeros((n, ROPE_HALF), F32)
    rc = jnp.tile(jnp.concatenate([cos, cos, one], axis=1), (1, 2))
    ra = jnp.tile(jnp.concatenate([-sin, z8, zero], axis=1), (1, 2))
    rb = jnp.tile(jnp.concatenate([z8, sin, zero], axis=1), (1, 2))
    return rc, ra, rb


def _lane_pad(vec, offset):
    out = jnp.zeros((1, LANES), F32)
    return out.at[0, offset:offset + vec.shape[0]].set(vec.astype(F32))


def kernel(x, c, positions, mod1_w, mod1_b, w_in, gm_ln_g, gm_ln_b, gm_w, gm_b, s5_lam_re, s5_lam_im, s5_log_step, s5_b_re, s5_b_im, s5_c_re, s5_c_im, s5_d, s5_glu_w, s5_glu_b, ssd_conv_w, ssd_conv_b, ssd_dt_bias, ssd_a_log, ssd_d, ssd_norm_g, w_branch, w_gate, w_out, ln1_g, ln1_b, mod2_w, mod2_b, router_w, router_bias, exp_w1, exp_w3, exp_w2, sh_w1, sh_w3, sh_w2, ln2_g, ln2_b):
    bsz, seq, d = x.shape
    n = bsz * seq
    depth = w_in.shape[0]
    x2 = x.reshape(n, d)

    c_pad = jnp.zeros((SUBLANES, d), F32).at[:bsz].set(c)
    mods1 = adaln_all(c_pad, mod1_w, mod1_b[:, None, :])[:, :, None, :]
    mods2 = adaln_all(c_pad, mod2_w, mod2_b[:, None, :])[:, :, None, :]
    rc, ra, rb = _rope_tables(positions)

    tri = jnp.tril(jnp.ones((GM_CHUNK, GM_CHUNK), F32))
    per = N_EXPERTS // N_EXPERT_GROUPS
    eperm = np.array([(r % N_EXPERT_GROUPS) * per + r // N_EXPERT_GROUPS for r in range(N_EXPERTS)])
    kc = DSA_KEY_CHUNK
    L = S5_CHUNK
    n_chunks = seq // L

    for l in range(depth):
        ya, q, qi, kvk, aux, s5u, z, xbc = inproj(
            x2, mods1[l], _pack_w_in(w_in[l]), rc, ra, rb,
            gm_ln_g[l][None], gm_ln_b[l][None], (gm_w[l] * tri).astype(BF16),
            jnp.repeat(gm_b[l].T, BRANCH_W // GM_GROUPS, axis=1), seq, 512)

        vt = kvk[:, KV_V:KV_V + ATT_HEAD_DIM].reshape(bsz, seq // kc, kc, ATT_HEAD_DIM).transpose(0, 1, 3, 2)
        yb = dsa(qi, q, aux, kvk, vt, bsz, seq, kc=kc)

        w5, e5, al5 = s5_tables(s5_lam_re[l], s5_lam_im[l], s5_log_step[l], s5_b_re[l], s5_b_im[l],
                                s5_c_re[l], s5_c_im[l])
        u_g = (s5u.astype(BF16).reshape(bsz, n_chunks, L, S5_GROUPS, S5_GROUP)
               .transpose(3, 1, 0, 2, 4).reshape(S5_GROUPS, n_chunks * bsz, L * S5_GROUP))
        y_g = s5_scan(u_g, w5, e5, al5, n_chunks, bsz)
        yc = (y_g.reshape(S5_GROUPS, n_chunks, bsz, L, S5_GROUP).transpose(2, 1, 3, 0, 4).reshape(n, BRANCH_W))

        yd = ssd(z, xbc, aux, ssd_conv_w[l], ssd_conv_b[l][None],
                 _lane_pad(ssd_dt_bias[l], KV_DT - 128), _lane_pad(-jnp.exp(ssd_a_log[l]), KV_DT - 128),
                 jnp.repeat(ssd_d[l], SSD_HEAD_DIM)[None], ssd_norm_g[l][None], bsz, seq)

        x2 = merge(x2, mods1[l], ya, yb, yc, s5u, yd, s5_d[l][None], s5_glu_w[l].astype(BF16), s5_glu_b[l][None],
                   w_gate[l].astype(BF16), w_branch[l].astype(BF16), w_out[l].astype(BF16),
                   ln1_g[l][None], ln1_b[l][None], seq)

        rw = jnp.zeros((d, LANES), F32).at[:, :N_EXPERTS].set(router_w[l][:, eperm]).astype(BF16)
        x2 = moe(x2, mods2[l], rw, router_bias[l][eperm][:, None],
                 jnp.concatenate([exp_w1[l], exp_w3[l]], axis=2).astype(BF16), exp_w2[l].astype(BF16),
                 jnp.concatenate([sh_w1[l], sh_w3[l]], axis=1).astype(BF16), sh_w2[l].astype(BF16),
                 ln2_g[l][None], ln2_b[l][None], seq)

    return x2.reshape(bsz, seq, d)
```

```python
import functools
import math

import jax
import jax.numpy as jnp
import numpy as np
from jax import lax
from jax.experimental import pallas as pl
from jax.experimental.pallas import tpu as pltpu

F32 = jnp.float32
BF16 = jnp.bfloat16

D_MODEL = 1024
DEPTH = 2
BRANCH_W = 256
GM_CHUNK = 128
GM_GROUPS = 4
ATT_HEAD_DIM = 64
ATT_HEADS = 4
IDX_HEADS = 4
IDX_DIM = 64
TOPK_MAX = 256
ROPE_THETA = 500000.0
ROPE_HALF = 8
S5_GROUP = 16
S5_GROUPS = 16
S5_STATE = 64
S5_CHUNK = 16
S5_PACK = 8
S5_CHUNKS_PER_ITER = 8
S5_ROW_TILE = 512
SSD_HEAD_DIM = 64
SSD_HEADS = 4
SSD_GROUPS = 2
SSD_STATE = 64
SSD_CONV = 4
SSD_CHUNK = 128
SSD_XBC = 512
N_EXPERTS = 32
TOP_K = 8
N_EXPERT_GROUPS = 8
TOPK_GROUPS = 4
EXPERT_FF = 256
ROUTED_SCALE = 2.5
DN_ALPHA = (2 * DEPTH) ** 0.25
LN_EPS = 1e-5
RMS_EPS = 1e-6

LANES = 128
SUBLANES = 8
VMEM_LIMIT_BYTES = 56 * 1024 * 1024

C_GMU, C_GMV, C_Q, C_QI, C_KV, C_S5, C_Z, C_XBC = 0, 256, 512, 768, 1024, 1280, 1536, 1792
D_IN_PACKED = 2304
KV_K, KV_V, KV_KI, KV_WI, KV_DT = 0, 64, 128, 192, 196

NEG_BIG = -1e30
INT_MIN = -2 ** 31


def _cparams(*sem):
    return pltpu.CompilerParams(dimension_semantics=sem, vmem_limit_bytes=VMEM_LIMIT_BYTES)


def _const_spec(shape):
    nd = len(shape)
    return pl.BlockSpec(shape, lambda *_: (0,) * nd, pipeline_mode=pl.Buffered(1))


def _layer_norm(v, g, b):
    mu = jnp.mean(v, axis=-1, keepdims=True)
    d = v - mu
    var = jnp.mean(d * d, axis=-1, keepdims=True)
    return d * lax.rsqrt(var + LN_EPS) * g + b


def _sigmoid(v):
    return 1.0 / (1.0 + jnp.exp(-v))


def _silu(v):
    return v * _sigmoid(v)


def _adaln_kernel(c_ref, w_ref, b_ref, o_ref):
    o_ref[...] = jnp.dot(c_ref[...], w_ref[...], preferred_element_type=F32,
                         precision=lax.Precision.HIGHEST) + b_ref[...]


def adaln_all(c_pad, w_all, b_all):
    m, d, d3 = w_all.shape
    bn = 512
    return pl.pallas_call(
        _adaln_kernel,
        grid=(m, d3 // bn),
        in_specs=[pl.BlockSpec((SUBLANES, d), lambda i, j: (0, 0)),
                  pl.BlockSpec((None, d, bn), lambda i, j: (i, 0, j)),
                  pl.BlockSpec((None, 1, bn), lambda i, j: (i, 0, j))],
        out_specs=pl.BlockSpec((None, SUBLANES, bn), lambda i, j: (i, 0, j)),
        out_shape=jax.ShapeDtypeStruct((m, SUBLANES, d3), F32),
        compiler_params=_cparams("parallel", "parallel"),
        name="adaln",
    )(c_pad, w_all, b_all)


def _rope128(t, c, a, b):
    return t * c + pltpu.roll(t, LANES - ROPE_HALF, axis=1) * a + pltpu.roll(t, ROPE_HALF, axis=1) * b


def _inproj_kernel(x_ref, shift_ref, scale_ref, w_ref, rc_ref, ra_ref, rb_ref,
                   lng_ref, lnb_ref, gmw_ref, gmb_ref,
                   ya_ref, q_ref, qi_ref, kvk_ref, aux_ref, s5u_ref, z_ref, xbc_ref):
    t = x_ref.shape[0]
    h = (x_ref[...] * (1.0 + scale_ref[...]) + shift_ref[...]).astype(BF16)
    rc, ra, rb = rc_ref[...], ra_ref[...], rb_ref[...]

    def proj(c0, width):
        return jnp.dot(h, w_ref[:, c0:c0 + width], preferred_element_type=F32)

    for c0, o_ref, mul in ((C_Q, q_ref, ATT_HEAD_DIM ** -0.5 * math.log2(math.e)), (C_QI, qi_ref, 1.0)):
        p = proj(c0, 256)
        o_ref[:, 0:128] = (_rope128(p[:, 0:128], rc, ra, rb) * mul).astype(BF16)
        o_ref[:, 128:256] = (_rope128(p[:, 128:256], rc, ra, rb) * mul).astype(BF16)

    p = proj(C_KV, 256)
    lane = lax.broadcasted_iota(jnp.int32, (t, LANES), 1)
    first = lane < ATT_HEAD_DIM
    rc1 = jnp.where(first, rc, 1.0)
    ra1 = jnp.where(first, ra, 0.0)
    rb1 = jnp.where(first, rb, 0.0)
    kv = _rope128(p[:, 0:128], rc1, ra1, rb1)
    kiw = _rope128(p[:, 128:256], rc1, ra1, rb1)
    kvk_ref[:, 0:128] = kv.astype(BF16)
    kvk_ref[:, 128:256] = kiw.astype(BF16)
    aux_ref[...] = kiw

    s5u_ref[...] = proj(C_S5, 256)
    z_ref[...] = proj(C_Z, 256)
    xbc_ref[...] = proj(C_XBC, 512)

    u = jax.nn.gelu(proj(C_GMU, 256))
    v = _layer_norm(jax.nn.gelu(proj(C_GMV, 256)), lng_ref[...], lnb_ref[...]).astype(BF16)
    gw = GM_CHUNK
    cw = BRANCH_W // GM_GROUPS
    for ci in range(t // gw):
        rows = slice(ci * gw, (ci + 1) * gw)
        for g in range(GM_GROUPS):
            cols = slice(g * cw, (g + 1) * cw)
            sv = jnp.dot(gmw_ref[g], v[rows, cols], preferred_element_type=F32) + gmb_ref[:, cols]
            ya_ref[rows, cols] = (u[rows, cols] * sv).astype(BF16)


def inproj(x2, mod, w_in_p, rc, ra, rb, lng, lnb, gmw, gmb, seq, tile):
    n, d = x2.shape
    tpb = seq // tile
    row = lambda width: pl.BlockSpec((tile, width), lambda i: (i, 0))
    outs = [(256, BF16), (256, BF16), (256, BF16), (256, BF16), (128, F32), (256, F32), (256, F32), (512, F32)]
    return pl.pallas_call(
        _inproj_kernel,
        grid=(n // tile,),
        in_specs=[row(d),
                  pl.BlockSpec((None, 1, d), lambda i: (i // tpb, 0, 0)),
                  pl.BlockSpec((None, 1, d), lambda i: (i // tpb, 0, 1)),
                  _const_spec((d, D_IN_PACKED)),
                  row(LANES), row(LANES), row(LANES),
                  _const_spec((1, 256)), _const_spec((1, 256)),
                  _const_spec((GM_GROUPS, GM_CHUNK, GM_CHUNK)), _const_spec((GM_CHUNK, 256))],
        out_specs=[row(w) for w, _ in outs],
        out_shape=[jax.ShapeDtypeStruct((n, w), dt) for w, dt in outs],
        compiler_params=_cparams("parallel"),
        name="inproj",
    )(x2, mod, mod, w_in_p, rc, ra, rb, lng, lnb, gmw, gmb)


def _dsa_kernel(qi_ref, q_ref, aux_ref, kvk_ref, vt_ref, o_ref, key_ref, planes_ref, mask_ref, lg0_ref, lg1_ref,
                *, tq, kc, n_sel):
    j = pl.program_id(1)
    nh = ATT_HEADS
    n_it = ((j + 1) * tq + 2 * kc - 1) // (2 * kc)
    n_rows = n_it * 2 * kc
    nt = (((1,), (1,)), ((), ()))

    @pl.when((pl.program_id(0) == 0) & (j == 0))
    def _():
        planes_ref[...] = jnp.zeros_like(planes_ref)

    qih = jnp.concatenate([qi_ref[:, h * IDX_DIM:(h + 1) * IDX_DIM] for h in range(IDX_HEADS)], axis=0)
    qh = jnp.concatenate([q_ref[:, h * ATT_HEAD_DIM:(h + 1) * ATT_HEAD_DIM] for h in range(nh)], axis=0)
    aux_t = aux_ref[...].T
    w_scale = IDX_HEADS ** -0.5 * IDX_DIM ** -0.5
    wrow = [aux_t[KV_WI - 128 + h:KV_WI - 128 + h + 1, :] * w_scale for h in range(IDX_HEADS)]

    qpos = j * tq + lax.broadcasted_iota(jnp.int32, (kc, tq), 1)
    krel = lax.broadcasted_iota(jnp.int32, (kc, tq), 0)

    last_k0 = kvk_ref.shape[0] - kc

    def logits(col0, q_rows, k0):
        kk = kvk_ref[pl.ds(pl.multiple_of(jnp.minimum(k0, last_k0), kc), kc), col0:col0 + IDX_DIM]
        return lax.dot_general(kk, q_rows, nt, preferred_element_type=F32)

    def pipelined(col0, q_rows, consume, carry):
        lg0_ref[...] = logits(col0, q_rows, 0)

        def pair(i, carry):
            k0 = pl.multiple_of(i * 2 * kc, 2 * kc)
            lg1_ref[...] = logits(col0, q_rows, k0 + kc)
            carry = consume(lg0_ref, k0, 2 * i, carry)
            lg0_ref[...] = logits(col0, q_rows, k0 + 2 * kc)
            return consume(lg1_ref, k0 + kc, 2 * i + 1, carry)

        return lax.fori_loop(0, n_it, pair, carry)

    def score_chunk(lg_ref, k0, c, carry):
        sc = jnp.zeros((kc, tq), F32)
        for h in range(IDX_HEADS):
            sc = sc + jnp.maximum(lg_ref[:, h * tq:(h + 1) * tq], 0.0) * wrow[h]
        sc = jnp.where(sc == 0.0, 0.0, sc)
        bits = lax.bitcast_convert_type(sc, jnp.int32)
        key = bits ^ ((bits >> 31) & jnp.int32(0x7FFFFFFF))
        key_ref[pl.ds(k0, kc), :] = jnp.where(k0 + krel <= qpos, key, jnp.int32(INT_MIN))
        return carry

    pipelined(KV_KI, qih, score_chunk, 0)

    wb = 32
    sbr = wb * SUBLANES

    def to_planes(sb, carry):
        r0 = pl.multiple_of(sb * sbr, sbr)
        rows = pl.ds(pl.multiple_of(sb * SUBLANES, SUBLANES), SUBLANES)
        for l0 in range(0, tq, LANES):
            cols = slice(l0, l0 + LANES)
            a = [key_ref[pl.ds(r0 + SUBLANES * w, SUBLANES), cols] ^ jnp.int32(INT_MIN) for w in range(wb)]
            half, m = wb // 2, 0x0000FFFF
            while half:
                k = 0
                while k < wb:
                    t = (a[k] ^ lax.shift_right_logical(a[k + half], half)) & jnp.int32(np.uint32(m).astype(np.int32))
                    a[k] = a[k] ^ t
                    a[k + half] = a[k + half] ^ (t << half)
                    k = (k + half + 1) & ~half
                half >>= 1
                m ^= (m << half) & 0xFFFFFFFF
            for p in range(wb):
                planes_ref[p, rows, cols] = a[p]
        return carry

    lax.fori_loop(0, n_rows // sbr, to_planes, 0)

    pr = planes_ref.shape[1]
    prow = lax.broadcasted_iota(jnp.int32, (pr, tq), 0)
    live0 = jnp.where(prow < n_rows // wb, jnp.int32(-1), jnp.int32(0))

    def radix_step(live, remaining, plane):
        ones = jnp.sum(lax.population_count(live & plane), axis=0, keepdims=True)
        take = ones >= remaining
        remaining = jnp.where(take, remaining, remaining - ones)
        live = live & (plane ^ jnp.where(take, 0, -1))
        return live, remaining, take

    def score_pass(p, carry):
        live, remaining, thr_u = carry
        live, remaining, take = radix_step(live, remaining, planes_ref[p])
        thr_u = thr_u | jnp.where(take, lax.shift_right_logical(jnp.int32(INT_MIN), p), 0)
        return live, remaining, thr_u

    live, remaining, thr_u = lax.fori_loop(
        0, wb, score_pass, (live0, jnp.full((1, tq), n_sel, jnp.int32), jnp.zeros((1, tq), jnp.int32)))
    thr = thr_u ^ jnp.int32(INT_MIN)

    n_tiles = pr // SUBLANES
    tile_inv = (n_tiles - 1) - (prow >> 3)
    sub_inv = (SUBLANES - 1) - (prow & (SUBLANES - 1))
    lane_bit = [0xAAAAAAAA, 0xCCCCCCCC, 0xF0F0F0F0, 0xFF00FF00, 0xFFFF0000]
    inv_planes = ([(8 + t, -((tile_inv >> t) & 1)) for t in range(max(n_tiles - 1, 1).bit_length())]
                  + [(3 + t, jnp.full((pr, tq), np.uint32(lane_bit[t]).astype(np.int32), jnp.int32)) for t in range(5)]
                  + [(t, -((sub_inv >> t) & 1)) for t in range(3)])
    inv_cut = jnp.zeros((1, tq), jnp.int32)
    for bit, plane in sorted(inv_planes, key=lambda e: -e[0]):
        live, remaining, take = radix_step(live, remaining, plane)
        inv_cut = inv_cut | jnp.where(take, 1 << bit, 0)
    row_cut = (pr * wb - 1) - inv_cut

    tb = 256
    rrel = lax.broadcasted_iota(jnp.int32, (tb, tq), 0)

    def flag_rows(i, carry):
        r0 = pl.multiple_of(i * tb, tb)
        blk = key_ref[pl.ds(r0, tb), :]
        keep = ((blk > thr) | ((blk == thr) & (r0 + rrel <= row_cut))) & (blk > INT_MIN)
        mask_ref[pl.ds(r0, tb), :] = jnp.where(keep, 0.0, NEG_BIG)
        return carry

    lax.fori_loop(0, n_rows // tb, flag_rows, 0)

    def att_chunk(lg_ref, k0, c, carry):
        m, l, acc = carry
        madd = mask_ref[pl.ds(k0, kc), :]
        m_parts, p_parts = [], []
        for h in range(nh):
            hs = slice(h * tq, (h + 1) * tq)
            lgh = lg_ref[:, hs] + madd
            mh = jnp.maximum(m[:, hs], jnp.max(lgh, axis=0, keepdims=True))
            m_parts.append(mh)
            p_parts.append(jnp.exp2(lgh - mh))
        m_new = jnp.concatenate(m_parts, axis=1)
        p = jnp.concatenate(p_parts, axis=1)
        alpha = jnp.exp2(m - m_new)
        l = l * alpha + jnp.sum(p, axis=0, keepdims=True)
        acc = acc * alpha + jnp.dot(vt_ref[c], p.astype(BF16), preferred_element_type=F32)
        return m_new, l, acc

    m0 = jnp.full((1, nh * tq), NEG_BIG, F32)
    l0 = jnp.zeros((1, nh * tq), F32)
    a0 = jnp.zeros((ATT_HEAD_DIM, nh * tq), F32)
    _, l, acc = pipelined(KV_K, qh, att_chunk, (m0, l0, a0))
    o = acc / l
    o = jnp.concatenate([o, jnp.zeros_like(o)], axis=0)
    for h in range(nh):
        o_ref[:, h * ATT_HEAD_DIM:(h + 1) * ATT_HEAD_DIM] = (
            o[:, h * tq:(h + 1) * tq].T[:, :ATT_HEAD_DIM].astype(o_ref.dtype))


DSA_KEY_CHUNK = 256


DSA_QUERY_TILE = 256


def dsa(qi, q, aux, kvk, vt, batch, seq, tq=DSA_QUERY_TILE, kc=DSA_KEY_CHUNK):
    n = qi.shape[0]
    nq = seq // tq
    n_sel = min(TOPK_MAX, seq // 4)
    blk = lambda w: pl.BlockSpec((tq, w), lambda b, j: (b * nq + j, 0))
    return pl.pallas_call(
        functools.partial(_dsa_kernel, tq=tq, kc=kc, n_sel=n_sel),
        grid=(batch, nq),
        in_specs=[blk(256), blk(256), blk(LANES),
                  pl.BlockSpec((seq, 256), lambda b, j: (b, 0), pipeline_mode=pl.Buffered(1)),
                  pl.BlockSpec((None, seq // kc, ATT_HEAD_DIM, kc), lambda b, j: (b, 0, 0, 0),
                               pipeline_mode=pl.Buffered(1))],
        out_specs=blk(256),
        out_shape=jax.ShapeDtypeStruct((n, 256), BF16),
        scratch_shapes=[pltpu.VMEM((seq, tq), jnp.int32),
                        pltpu.VMEM((32, seq // 32, tq), jnp.int32),
                        pltpu.VMEM((seq, tq), F32),
                        pltpu.VMEM((kc, ATT_HEADS * tq), F32),
                        pltpu.VMEM((kc, ATT_HEADS * tq), F32)],
        compiler_params=_cparams("arbitrary", "arbitrary"),
        name="dsa",
    )(qi, q, aux, kvk, vt)


def s5_tables(lam_re, lam_im, log_step, b_re, b_im, c_re, c_im):
    hp = lax.Precision.HIGHEST
    L = S5_CHUNK
    g_, p_ = lam_re.shape
    step = jnp.exp(log_step)[:, None]
    d = jnp.arange(L + 1, dtype=F32)[None, :, None]
    mag = jnp.exp(d * (lam_re * step)[:, None, :])
    ang = d * (lam_im * step)[:, None, :]
    pw_re, pw_im = mag * jnp.cos(ang), mag * jnp.sin(ang)
    ab_re, ab_im = pw_re[:, 1], pw_im[:, 1]
    den = lam_re * lam_re + lam_im * lam_im
    nr = ab_re - 1.0
    cr = (nr * lam_re + ab_im * lam_im) / den
    ci = (ab_im * lam_re - nr * lam_im) / den
    bb_re = cr[..., None] * b_re - ci[..., None] * b_im
    bb_im = cr[..., None] * b_im + ci[..., None] * b_re
    cp_re = c_re[:, None] * pw_re[:, :, None, :] - c_im[:, None] * pw_im[:, :, None, :]
    cp_im = c_re[:, None] * pw_im[:, :, None, :] + c_im[:, None] * pw_re[:, :, None, :]
    kern = (jnp.einsum('gdop,gpi->gdoi', cp_re, bb_re, precision=hp)
            - jnp.einsum('gdop,gpi->gdoi', cp_im, bb_im, precision=hp))
    gp = S5_PACK
    npk = g_ // gp
    pk = lambda a: a.reshape((npk, gp) + a.shape[1:])

    def block_diag(a, ax_g, ax_h):
        rest = a.shape[2:]
        out_shape = list((npk,) + rest)
        for ax in sorted((ax_g, ax_h)):
            out_shape.insert(ax, gp)
        out = jnp.zeros(out_shape, a.dtype)
        for g in range(gp):
            idx = [slice(None)] * len(out_shape)
            idx[ax_g] = g
            idx[ax_h] = g
            out = out.at[tuple(idx)].set(a[:, g])
        return out

    kt = kern[:, :L].transpose(0, 1, 3, 2)
    toe = jnp.stack([jnp.pad(kt, ((0, 0), (s, 0), (0, 0), (0, 0)))[:, :L] for s in range(L)], axis=1)
    toe = block_diag(pk(toe.transpose(0, 1, 3, 2, 4)), 2, 5)
    toe = toe.reshape(npk, L * gp * S5_GROUP, L * gp * S5_GROUP)
    pw_rev_re, pw_rev_im = pw_re[:, L - 1::-1][:, :L], pw_im[:, L - 1::-1][:, :L]
    bt_re, bt_im = bb_re.transpose(0, 2, 1)[:, None], bb_im.transpose(0, 2, 1)[:, None]
    gm_re = pw_rev_re[:, :, None, :] * bt_re - pw_rev_im[:, :, None, :] * bt_im
    gm_im = pw_rev_re[:, :, None, :] * bt_im + pw_rev_im[:, :, None, :] * bt_re
    place = lambda a: block_diag(pk(a), 2, 4).reshape(npk, L * gp * S5_GROUP, gp * p_)
    w = jnp.concatenate([toe, place(gm_re), place(gm_im)], axis=2).astype(BF16)
    e_re = cp_re[:, 1:].transpose(0, 3, 1, 2)
    e_im = -cp_im[:, 1:].transpose(0, 3, 1, 2)
    read = lambda a: block_diag(pk(a), 1, 4).reshape(npk, gp * p_, L * gp * S5_GROUP)
    e = jnp.concatenate([read(e_re), read(e_im)], axis=1).astype(BF16)
    al = jnp.stack([pw_re[:, L].reshape(npk, gp * p_), pw_im[:, L].reshape(npk, gp * p_)], axis=1)
    return w, e, al


def _s5_kernel(u_ref, w_ref, e_ref, al_ref, y_ref, ys_ref, hp_ref, state_ref, *, n_chunks, nb):
    lw = u_ref.shape[1]
    p_ = al_ref.shape[1]

    @pl.when(pl.program_id(1) == 0)
    def _():
        state_ref[...] = jnp.zeros_like(state_ref)

    ys_ref[...] = jnp.dot(u_ref[...], w_ref[...], preferred_element_type=F32)
    ar = al_ref[0:1, :]
    ai = al_ref[1:2, :]

    cpi = S5_CHUNKS_PER_ITER
    rows_per_iter = cpi * nb
    assert rows_per_iter % SUBLANES == 0 and n_chunks % cpi == 0

    def step(it, carry):
        hr, hi = carry
        rows = pl.ds(pl.multiple_of(it * rows_per_iter, rows_per_iter), rows_per_iter)
        st = ys_ref[rows, lw:lw + 2 * p_]
        hrs, his = [], []
        for k in range(cpi):
            hrs.append(hr)
            his.append(hi)
            sr, si = st[k * nb:(k + 1) * nb, 0:p_], st[k * nb:(k + 1) * nb, p_:2 * p_]
            hr, hi = ar * hr - ai * hi + sr, ar * hi + ai * hr + si
        hp_ref[rows, 0:p_] = jnp.concatenate(hrs, axis=0)
        hp_ref[rows, p_:2 * p_] = jnp.concatenate(his, axis=0)
        return hr, hi

    hr, hi = lax.fori_loop(0, n_chunks // cpi, step, (state_ref[0], state_ref[1]))
    state_ref[0] = hr
    state_ref[1] = hi
    y_ref[...] = (ys_ref[:, 0:lw] + jnp.dot(hp_ref[...].astype(BF16), e_ref[...], preferred_element_type=F32)
                  ).astype(y_ref.dtype)


def s5_scan(u_p, w, e, al, n_chunks, nb):
    npk, r, lw = u_p.shape
    sw = al.shape[2]
    n_tiles = max(1, r // S5_ROW_TILE)
    rt = r // n_tiles
    return pl.pallas_call(
        functools.partial(_s5_kernel, n_chunks=n_chunks // n_tiles, nb=nb),
        grid=(npk, n_tiles),
        in_specs=[pl.BlockSpec((None, rt, lw), lambda g, i: (g, i, 0)),
                  pl.BlockSpec((None, lw, lw + 2 * sw), lambda g, i: (g, 0, 0), pipeline_mode=pl.Buffered(1)),
                  pl.BlockSpec((None, 2 * sw, lw), lambda g, i: (g, 0, 0), pipeline_mode=pl.Buffered(1)),
                  pl.BlockSpec((None, 2, sw), lambda g, i: (g, 0, 0))],
        out_specs=pl.BlockSpec((None, rt, lw), lambda g, i: (g, i, 0)),
        out_shape=jax.ShapeDtypeStruct((npk, r, lw), BF16),
        scratch_shapes=[pltpu.VMEM((rt, lw + 2 * sw), F32), pltpu.VMEM((rt, 2 * sw), F32),
                        pltpu.VMEM((2, nb, sw), F32)],
        compiler_params=_cparams("arbitrary", "arbitrary"),
        name="s5",
    )(u_p, w, e, al)


def _ssd_kernel(z_ref, xbc_ref, aux_ref, cw_ref, cb_ref, dtb_ref, aneg_ref, dsk_ref, ng_ref,
                y_ref, state_ref, halo_ref, *, tile):
    hp = lax.Precision.HIGHEST
    ck = SSD_CHUNK
    hd = SSD_HEAD_DIM

    @pl.when(pl.program_id(1) == 0)
    def _():
        state_ref[...] = jnp.zeros_like(state_ref)
        halo_ref[...] = jnp.zeros_like(halo_ref)

    xbc = xbc_ref[...]
    ext = jnp.concatenate([halo_ref[...], xbc], axis=0)
    halo_ref[...] = xbc[tile - SUBLANES:tile, :]
    conv = cb_ref[...] + sum(cw_ref[jj:jj + 1, :] * ext[SUBLANES - (SSD_CONV - 1) + jj:SUBLANES - (SSD_CONV - 1) + jj + tile, :]
                             for jj in range(SSD_CONV))
    act = _silu(conv)
    x = act[:, 0:BRANCH_W]
    bm = act[:, BRANCH_W:BRANCH_W + SSD_GROUPS * SSD_STATE]
    cm = act[:, BRANCH_W + SSD_GROUPS * SSD_STATE:]
    dt_all = jax.nn.softplus(aux_ref[...] + dtb_ref[...])
    adt_all = dt_all * aneg_ref[...]

    r_i = lax.broadcasted_iota(jnp.int32, (ck, ck), 0)
    c_i = lax.broadcasted_iota(jnp.int32, (ck, ck), 1)
    lower = c_i <= r_i
    tri = lower.astype(F32)
    zs = _silu(z_ref[...])
    ng = ng_ref[...]
    dl = KV_DT - 128

    for ci in range(tile // ck):
        rows = slice(ci * ck, (ci + 1) * ck)
        a_cum = jnp.dot(tri, adt_all[rows], preferred_element_type=F32, precision=hp)
        a_cum_t = a_cum.T
        bm_t = bm[rows].T
        cmb = cm[rows].astype(BF16)
        bmb = bm[rows].astype(BF16)
        gmat = [lax.dot_general(cmb[:, g * SSD_STATE:(g + 1) * SSD_STATE], bmb[:, g * SSD_STATE:(g + 1) * SSD_STATE],
                                (((1,), (1,)), ((), ())), preferred_element_type=F32) for g in range(SSD_GROUPS)]
        ys = []
        for h in range(SSD_HEADS):
            g = h // (SSD_HEADS // SSD_GROUPS)
            acol = a_cum[:, dl + h:dl + h + 1]
            arow = a_cum_t[dl + h:dl + h + 1, :]
            dtc = dt_all[rows, dl + h:dl + h + 1]
            xh = x[rows, h * hd:(h + 1) * hd]
            xdt = xh * dtc
            lmat = jnp.exp(jnp.where(lower, acol - arow, NEG_BIG))
            y_diag = jnp.dot((gmat[g] * lmat).astype(BF16), xdt.astype(BF16), preferred_element_type=F32)
            s_prev = state_ref[h]
            y_off = jnp.exp(acol) * jnp.dot(cmb[:, g * SSD_STATE:(g + 1) * SSD_STATE], s_prev.astype(BF16),
                                            preferred_element_type=F32)
            a_last = a_cum[ck - 1:ck, dl + h:dl + h + 1]
            decay = jnp.exp(a_last - acol)
            s_new = jnp.dot(bm_t[g * SSD_STATE:(g + 1) * SSD_STATE, :].astype(BF16), (xdt * decay).astype(BF16),
                            preferred_element_type=F32)
            state_ref[h] = jnp.exp(a_last) * s_prev + s_new
            ys.append(y_diag + y_off + dsk_ref[:, h * hd:(h + 1) * hd] * xh)
        y = jnp.concatenate(ys, axis=1) * zs[rows]
        gw = BRANCH_W // SSD_GROUPS
        outs = []
        for g in range(SSD_GROUPS):
            yg = y[:, g * gw:(g + 1) * gw]
            outs.append(yg * lax.rsqrt(jnp.mean(yg * yg, axis=-1, keepdims=True) + RMS_EPS))
        y_ref[rows, :] = (jnp.concatenate(outs, axis=1) * ng).astype(y_ref.dtype)


def ssd(z, xbc, aux, cw, cb, dtb, aneg, dsk, ng, batch, seq, tile=512):
    n = z.shape[0]
    nt = seq // tile
    row = lambda w: pl.BlockSpec((tile, w), lambda b, i: (b * nt + i, 0))
    return pl.pallas_call(
        functools.partial(_ssd_kernel, tile=tile),
        grid=(batch, nt),
        in_specs=[row(256), row(512), row(LANES),
                  _const_spec((SSD_CONV, SSD_XBC)), _const_spec((1, SSD_XBC)), _const_spec((1, LANES)),
                  _const_spec((1, LANES)), _const_spec((1, 256)), _const_spec((1, 256))],
        out_specs=row(256),
        out_shape=jax.ShapeDtypeStruct((n, 256), BF16),
        scratch_shapes=[pltpu.VMEM((SSD_HEADS, SSD_STATE, SSD_HEAD_DIM), F32), pltpu.VMEM((SUBLANES, SSD_XBC), F32)],
        compiler_params=_cparams("parallel", "arbitrary"),
        name="ssd",
    )(z, xbc, aux, cw, cb, dtb, aneg, dsk, ng)


def _merge_kernel(x_ref, shift_ref, scale_ref, gate_ref, ya_ref, yb_ref, yc_ref, s5u_ref, yd_ref,
                  s5d_ref, gluw_ref, glub_ref, wg_ref, wb_ref, wo_ref, lng_ref, lnb_ref, o_ref):
    x = x_ref[...]
    h = (x * (1.0 + scale_ref[...]) + shift_ref[...]).astype(BF16)
    yc = jax.nn.gelu(yc_ref[...] + s5d_ref[...] * s5u_ref[...])
    yc = yc * _sigmoid(jnp.dot(yc.astype(BF16), gluw_ref[...], preferred_element_type=F32) + glub_ref[...])
    branches = (ya_ref[...], yb_ref[...], yc.astype(BF16), yd_ref[...])
    merged = None
    for i, br in enumerate(branches):
        gate = _sigmoid(jnp.dot(h, wg_ref[i], preferred_element_type=F32))
        term = gate * jnp.dot(br, wb_ref[i], preferred_element_type=F32)
        merged = term if merged is None else merged + term
    mix = jnp.dot(merged.astype(BF16), wo_ref[...], preferred_element_type=F32)
    o_ref[...] = _layer_norm(DN_ALPHA * x + (1.0 + gate_ref[...]) * mix, lng_ref[...], lnb_ref[...])


def merge(x2, mod, ya, yb, yc, s5u, yd, s5d, gluw, glub, wg, wb, wo, lng, lnb, seq, tile=512):
    n, d = x2.shape
    tpb = seq // tile
    row = lambda w: pl.BlockSpec((tile, w), lambda i: (i, 0))
    modspec = lambda k: pl.BlockSpec((None, 1, d), lambda i: (i // tpb, 0, k))
    return pl.pallas_call(
        _merge_kernel,
        grid=(n // tile,),
        in_specs=[row(d), modspec(0), modspec(1), modspec(2),
                  row(256), row(256), row(256), row(256), row(256),
                  _const_spec((1, 256)), _const_spec((256, 256)), _const_spec((1, 256)),
                  _const_spec((4, d, d)), _const_spec((4, 256, d)), _const_spec((d, d)),
                  _const_spec((1, d)), _const_spec((1, d))],
        out_specs=row(d),
        out_shape=jax.ShapeDtypeStruct((n, d), F32),
        compiler_params=_cparams("parallel"),
        name="merge",
    )(x2, mod, mod, mod, ya, yb, yc, s5u, yd, s5d, gluw, glub, wg, wb, wo, lng, lnb)


def _route(scores_t, bias_ref):
    ng = N_EXPERT_GROUPS
    per = N_EXPERTS // ng
    t = scores_t.shape[1]
    a = [scores_t[k * ng:(k + 1) * ng] for k in range(per)]
    b = [a[k] + bias_ref[k * ng:(k + 1) * ng, :] for k in range(per)]
    hi1, lo1 = jnp.maximum(b[0], b[1]), jnp.minimum(b[0], b[1])
    hi2, lo2 = jnp.maximum(b[2], b[3]), jnp.minimum(b[2], b[3])
    gs = jnp.maximum(hi1, hi2) + jnp.maximum(jnp.minimum(hi1, hi2), jnp.maximum(lo1, lo2))
    gidx = lax.broadcasted_iota(jnp.int32, (ng, t), 0)
    rank = jnp.zeros((ng, t), jnp.int32)
    for g2 in range(ng):
        row = gs[g2:g2 + 1, :]
        rank = rank + ((row > gs) | ((row == gs) & (g2 < gidx))).astype(jnp.int32)
    gmask = rank < TOPK_GROUPS
    neg_inf = jnp.float32(-jnp.inf)
    masked = [jnp.where(gmask, b[k], neg_inf) for k in range(per)]
    eidx = [gidx * per + k for k in range(per)]
    sel = [jnp.zeros((ng, t), jnp.bool_) for _ in range(per)]
    for _ in range(TOP_K):
        m = jnp.max(jnp.maximum(jnp.maximum(masked[0], masked[1]), jnp.maximum(masked[2], masked[3])),
                    axis=0, keepdims=True)
        cand = [jnp.where(masked[k] == m, eidx[k], N_EXPERTS) for k in range(per)]
        imin = jnp.min(jnp.minimum(jnp.minimum(cand[0], cand[1]), jnp.minimum(cand[2], cand[3])),
                       axis=0, keepdims=True)
        for k in range(per):
            hit = eidx[k] == imin
            sel[k] = sel[k] | hit
            masked[k] = jnp.where(hit, neg_inf, masked[k])
    w = [jnp.where(sel[k], a[k], 0.0) for k in range(per)]
    denom = jnp.sum(w[0] + w[1] + w[2] + w[3], axis=0, keepdims=True)
    gates = [w[k] / denom * ROUTED_SCALE for k in range(per)]
    return jnp.concatenate(gates + [jnp.zeros((LANES - N_EXPERTS, t), F32)], axis=0)


MOE_EXPERTS_PER_STEP = 4


def _moe_kernel(x_ref, shift_ref, scale_ref, gate_ref, rw_ref, rb_ref, *refs):
    eps = MOE_EXPERTS_PER_STEP
    w_refs = refs[:2 * eps]
    sw13_ref, sw2_ref, lng_ref, lnb_ref, o_ref, h_ref, acc_ref, g_ref = refs[2 * eps:]
    j = pl.program_id(1)
    ff = EXPERT_FF

    @pl.when(j == 0)
    def _():
        h = (x_ref[...] * (1.0 + scale_ref[...]) + shift_ref[...]).astype(BF16)
        h_ref[...] = h
        scores = _sigmoid(jnp.dot(h, rw_ref[...], preferred_element_type=F32))
        g_ref[...] = _route(scores.T, rb_ref).T
        ab = jnp.dot(h, sw13_ref[...], preferred_element_type=F32)
        act = _silu(ab[:, :ff]) * ab[:, ff:]
        acc_ref[...] = jnp.dot(act.astype(BF16), sw2_ref[...], preferred_element_type=F32)

    h = h_ref[...]
    lane = lax.broadcasted_iota(jnp.int32, g_ref.shape, 1)
    gates = g_ref[...]
    out = None
    for k in range(eps):
        w13_ref, w2_ref = w_refs[2 * k], w_refs[2 * k + 1]
        ab = jnp.dot(h, w13_ref[...], preferred_element_type=F32)
        gcol = jnp.sum(jnp.where(lane == eps * j + k, gates, 0.0), axis=1, keepdims=True)
        act = _silu(ab[:, :ff]) * ab[:, ff:] * gcol
        y = jnp.dot(act.astype(BF16), w2_ref[...], preferred_element_type=F32)
        out = y if out is None else out + y
    acc_ref[...] += out

    @pl.when(j == pl.num_programs(1) - 1)
    def _():
        o_ref[...] = _layer_norm(DN_ALPHA * x_ref[...] + (1.0 + gate_ref[...]) * acc_ref[...],
                                 lng_ref[...], lnb_ref[...])


def moe(x2, mod, rw, rb, w13, w2, sw13, sw2, lng, lnb, seq, tile=1024):
    n, d = x2.shape
    tpb = seq // tile
    ne = w13.shape[0]
    per = N_EXPERTS // N_EXPERT_GROUPS
    emap = lambda j: (j % N_EXPERT_GROUPS) * per + j // N_EXPERT_GROUPS
    row = pl.BlockSpec((tile, d), lambda i, j: (i, 0))
    modspec = lambda k: pl.BlockSpec((None, 1, d), lambda i, j: (i // tpb, 0, k))
    eps = MOE_EXPERTS_PER_STEP
    w13spec = lambda k: pl.BlockSpec((None, d, 2 * EXPERT_FF), lambda i, j: (emap(eps * j + k), 0, 0))
    w2spec = lambda k: pl.BlockSpec((None, EXPERT_FF, d), lambda i, j: (emap(eps * j + k), 0, 0))
    w_specs = [spec(k) for k in range(eps) for spec in (w13spec, w2spec)]
    return pl.pallas_call(
        _moe_kernel,
        grid=(n // tile, ne // eps),
        in_specs=[row, modspec(0), modspec(1), modspec(2),
                  _const_spec((d, LANES)), _const_spec((N_EXPERTS, 1)),
                  *w_specs,
                  _const_spec((d, 2 * EXPERT_FF)), _const_spec((EXPERT_FF, d)),
                  _const_spec((1, d)), _const_spec((1, d))],
        out_specs=row,
        out_shape=jax.ShapeDtypeStruct((n, d), F32),
        scratch_shapes=[pltpu.VMEM((tile, d), BF16), pltpu.VMEM((tile, d), F32), pltpu.VMEM((tile, LANES), F32)],
        compiler_params=_cparams("parallel", "arbitrary"),
        name="moe",
    )(x2, mod, mod, mod, rw, rb, *([w13, w2] * eps), sw13, sw2, lng, lnb)


def _pack_w_in(w):
    d = w.shape[0]
    cuts = np.cumsum([0, 256, 256, 256, 64, 64, 256, 64, 4, 256, 256, 512, 4])
    gm_u, gm_v, q, k, v, qi, ki, wi, s5, z, xbc, dt = [w[:, cuts[i]:cuts[i + 1]] for i in range(12)]
    pad = jnp.zeros((d, 256 - (64 * 3 + 8)), w.dtype)
    return jnp.concatenate([gm_u, gm_v, q, qi, k, v, ki, wi, dt, pad, s5, z, xbc], axis=1).astype(BF16)


def _rope_tables(positions):
    rot = 2 * ROPE_HALF
    inv = ROPE_THETA ** (-jnp.arange(0, rot, 2, dtype=F32) / rot)
    ang = positions.astype(F32).reshape(-1, 1) * inv
    cos, sin = jnp.cos(ang), jnp.sin(ang)
    n = ang.shape[0]
    rest = ATT_HEAD_DIM - rot
    one = jnp.ones((n, rest), F32)
    zero = jnp.zeros((n, rest), F32)
    z8 = jnp.zeros((n, ROPE_HALF), F32)
    rc = jnp.tile(jnp.concatenate([cos, cos, one], axis=1), (1, 2))
    ra = jnp.tile(jnp.concatenate([-sin, z8, zero], axis=1), (1, 2))
    rb = jnp.tile(jnp.concatenate([z8, sin, zero], axis=1), (1, 2))
    return rc, ra, rb


def _lane_pad(vec, offset):
    out = jnp.zeros((1, LANES), F32)
    return out.at[0, offset:offset + vec.shape[0]].set(vec.astype(F32))


def kernel(x, c, positions, mod1_w, mod1_b, w_in, gm_ln_g, gm_ln_b, gm_w, gm_b, s5_lam_re, s5_lam_im, s5_log_step, s5_b_re, s5_b_im, s5_c_re, s5_c_im, s5_d, s5_glu_w, s5_glu_b, ssd_conv_w, ssd_conv_b, ssd_dt_bias, ssd_a_log, ssd_d, ssd_norm_g, w_branch, w_gate, w_out, ln1_g, ln1_b, mod2_w, mod2_b, router_w, router_bias, exp_w1, exp_w3, exp_w2, sh_w1, sh_w3, sh_w2, ln2_g, ln2_b):
    bsz, seq, d = x.shape
    n = bsz * seq
    depth = w_in.shape[0]
    x2 = x.reshape(n, d)

    c_pad = jnp.zeros((SUBLANES, d), F32).at[:bsz].set(c)
    mods1 = adaln_all(c_pad, mod1_w, mod1_b[:, None, :])[:, :, None, :]
    mods2 = adaln_all(c_pad, mod2_w, mod2_b[:, None, :])[:, :, None, :]
    rc, ra, rb = _rope_tables(positions)

    tri = jnp.tril(jnp.ones((GM_CHUNK, GM_CHUNK), F32))
    per = N_EXPERTS // N_EXPERT_GROUPS
    eperm = np.array([(r % N_EXPERT_GROUPS) * per + r // N_EXPERT_GROUPS for r in range(N_EXPERTS)])
    kc = DSA_KEY_CHUNK
    L = S5_CHUNK
    n_chunks = seq // L

    for l in range(depth):
        ya, q, qi, kvk, aux, s5u, z, xbc = inproj(
            x2, mods1[l], _pack_w_in(w_in[l]), rc, ra, rb,
            gm_ln_g[l][None], gm_ln_b[l][None], (gm_w[l] * tri).astype(BF16),
            jnp.repeat(gm_b[l].T, BRANCH_W // GM_GROUPS, axis=1), seq, 512)

        vt = kvk[:, KV_V:KV_V + ATT_HEAD_DIM].reshape(bsz, seq // kc, kc, ATT_HEAD_DIM).transpose(0, 1, 3, 2)
        yb = dsa(qi, q, aux, kvk, vt, bsz, seq, kc=kc)

        w5, e5, al5 = s5_tables(s5_lam_re[l], s5_lam_im[l], s5_log_step[l], s5_b_re[l], s5_b_im[l],
                                s5_c_re[l], s5_c_im[l])
        npk = S5_GROUPS // S5_PACK
        u_p = (s5u.astype(BF16).reshape(bsz, n_chunks, L, npk, LANES)
               .transpose(3, 1, 0, 2, 4).reshape(npk, n_chunks * bsz, L * LANES))
        y_p = s5_scan(u_p, w5, e5, al5, n_chunks, bsz)
        yc = y_p.reshape(npk, n_chunks, bsz, L, LANES).transpose(2, 1, 3, 0, 4).reshape(n, BRANCH_W)

        yd = ssd(z, xbc, aux, ssd_conv_w[l], ssd_conv_b[l][None],
                 _lane_pad(ssd_dt_bias[l], KV_DT - 128), _lane_pad(-jnp.exp(ssd_a_log[l]), KV_DT - 128),
                 jnp.repeat(ssd_d[l], SSD_HEAD_DIM)[None], ssd_norm_g[l][None], bsz, seq)

        x2 = merge(x2, mods1[l], ya, yb, yc, s5u, yd, s5_d[l][None], s5_glu_w[l].astype(BF16), s5_glu_b[l][None],
                   w_gate[l].astype(BF16), w_branch[l].astype(BF16), w_out[l].astype(BF16),
                   ln1_g[l][None], ln1_b[l][None], seq)

        rw = jnp.zeros((d, LANES), F32).at[:, :N_EXPERTS].set(router_w[l][:, eperm]).astype(BF16)
        x2 = moe(x2, mods2[l], rw, router_bias[l][eperm][:, None],
                 jnp.concatenate([exp_w1[l], exp_w3[l]], axis=2).astype(BF16), exp_w2[l].astype(BF16),
                 jnp.concatenate([sh_w1[l], sh_w3[l]], axis=1).astype(BF16), sh_w2[l].astype(BF16),
                 ln2_g[l][None], ln2_b[l][None], seq)

    return x2.reshape(bsz, seq, d)
```

```python
import functools
import math

import jax
import jax.numpy as jnp
import numpy as np
from jax import lax
from jax.experimental import pallas as pl
from jax.experimental.pallas import tpu as pltpu

F32 = jnp.float32
BF16 = jnp.bfloat16

D_MODEL = 1024
DEPTH = 2
BRANCH_W = 256
GM_CHUNK = 128
GM_GROUPS = 4
ATT_HEAD_DIM = 64
ATT_HEADS = 4
IDX_HEADS = 4
IDX_DIM = 64
TOPK_MAX = 256
ROPE_THETA = 500000.0
ROPE_HALF = 8
S5_GROUP = 16
S5_GROUPS = 16
S5_STATE = 64
S5_CHUNK = 16
S5_PACK = 8
S5_CHUNKS_PER_ITER = 8
S5_ROW_TILE = 512
SSD_HEAD_DIM = 64
SSD_HEADS = 4
SSD_GROUPS = 2
SSD_STATE = 64
SSD_CONV = 4
SSD_CHUNK = 128
SSD_XBC = 512
N_EXPERTS = 32
TOP_K = 8
N_EXPERT_GROUPS = 8
TOPK_GROUPS = 4
EXPERT_FF = 256
ROUTED_SCALE = 2.5
DN_ALPHA = (2 * DEPTH) ** 0.25
LN_EPS = 1e-5
RMS_EPS = 1e-6

LANES = 128
SUBLANES = 8
VMEM_LIMIT_BYTES = 56 * 1024 * 1024

C_GMU, C_GMV, C_Q, C_QI, C_KV, C_S5, C_Z, C_XBC = 0, 256, 512, 768, 1024, 1280, 1536, 1792
D_IN_PACKED = 2304
KV_K, KV_V, KV_KI, KV_WI, KV_DT = 0, 64, 128, 192, 196

NEG_BIG = -1e30
INT_MIN = -2 ** 31


def _cparams(*sem):
    return pltpu.CompilerParams(dimension_semantics=sem, vmem_limit_bytes=VMEM_LIMIT_BYTES)


def _const_spec(shape):
    nd = len(shape)
    return pl.BlockSpec(shape, lambda *_: (0,) * nd, pipeline_mode=pl.Buffered(1))


def _layer_norm(v, g, b):
    mu = jnp.mean(v, axis=-1, keepdims=True)
    d = v - mu
    var = jnp.mean(d * d, axis=-1, keepdims=True)
    return d * lax.rsqrt(var + LN_EPS) * g + b


def _sigmoid(v):
    return 1.0 / (1.0 + jnp.exp(-v))


def _silu(v):
    return v * _sigmoid(v)


def _adaln_kernel(c_ref, w_ref, b_ref, o_ref):
    o_ref[...] = jnp.dot(c_ref[...], w_ref[...], preferred_element_type=F32,
                         precision=lax.Precision.HIGHEST) + b_ref[...]


def adaln_all(c_pad, w_all, b_all):
    m, d, d3 = w_all.shape
    bn = 512
    return pl.pallas_call(
        _adaln_kernel,
        grid=(m, d3 // bn),
        in_specs=[pl.BlockSpec((SUBLANES, d), lambda i, j: (0, 0)),
                  pl.BlockSpec((None, d, bn), lambda i, j: (i, 0, j)),
                  pl.BlockSpec((None, 1, bn), lambda i, j: (i, 0, j))],
        out_specs=pl.BlockSpec((None, SUBLANES, bn), lambda i, j: (i, 0, j)),
        out_shape=jax.ShapeDtypeStruct((m, SUBLANES, d3), F32),
        compiler_params=_cparams("parallel", "parallel"),
        name="adaln",
    )(c_pad, w_all, b_all)


def _rope128(t, c, a, b):
    return t * c + pltpu.roll(t, LANES - ROPE_HALF, axis=1) * a + pltpu.roll(t, ROPE_HALF, axis=1) * b


def _inproj_kernel(x_ref, shift_ref, scale_ref, w_ref, rc_ref, ra_ref, rb_ref,
                   lng_ref, lnb_ref, gmw_ref, gmb_ref,
                   ya_ref, q_ref, qi_ref, kvk_ref, aux_ref, s5u_ref, z_ref, xbc_ref):
    t = x_ref.shape[0]
    h = (x_ref[...] * (1.0 + scale_ref[...]) + shift_ref[...]).astype(BF16)
    rc, ra, rb = rc_ref[...], ra_ref[...], rb_ref[...]

    def proj(c0, width):
        return jnp.dot(h, w_ref[:, c0:c0 + width], preferred_element_type=F32)

    for c0, o_ref, mul in ((C_Q, q_ref, ATT_HEAD_DIM ** -0.5 * math.log2(math.e)), (C_QI, qi_ref, 1.0)):
        p = proj(c0, 256)
        o_ref[:, 0:128] = (_rope128(p[:, 0:128], rc, ra, rb) * mul).astype(BF16)
        o_ref[:, 128:256] = (_rope128(p[:, 128:256], rc, ra, rb) * mul).astype(BF16)

    p = proj(C_KV, 256)
    lane = lax.broadcasted_iota(jnp.int32, (t, LANES), 1)
    first = lane < ATT_HEAD_DIM
    rc1 = jnp.where(first, rc, 1.0)
    ra1 = jnp.where(first, ra, 0.0)
    rb1 = jnp.where(first, rb, 0.0)
    kv = _rope128(p[:, 0:128], rc1, ra1, rb1)
    kiw = _rope128(p[:, 128:256], rc1, ra1, rb1)
    kvk_ref[:, 0:128] = kv.astype(BF16)
    kvk_ref[:, 128:256] = kiw.astype(BF16)
    aux_ref[...] = kiw

    s5u_ref[...] = proj(C_S5, 256)
    z_ref[...] = proj(C_Z, 256)
    xbc_ref[...] = proj(C_XBC, 512)

    u = jax.nn.gelu(proj(C_GMU, 256))
    v = _layer_norm(jax.nn.gelu(proj(C_GMV, 256)), lng_ref[...], lnb_ref[...]).astype(BF16)
    gw = GM_CHUNK
    cw = BRANCH_W // GM_GROUPS
    for ci in range(t // gw):
        rows = slice(ci * gw, (ci + 1) * gw)
        for g in range(GM_GROUPS):
            cols = slice(g * cw, (g + 1) * cw)
            sv = jnp.dot(gmw_ref[g], v[rows, cols], preferred_element_type=F32) + gmb_ref[:, cols]
            ya_ref[rows, cols] = (u[rows, cols] * sv).astype(BF16)


def inproj(x2, mod, w_in_p, rc, ra, rb, lng, lnb, gmw, gmb, seq, tile):
    n, d = x2.shape
    tpb = seq // tile
    row = lambda width: pl.BlockSpec((tile, width), lambda i: (i, 0))
    outs = [(256, BF16), (256, BF16), (256, BF16), (256, BF16), (128, F32), (256, F32), (256, F32), (512, F32)]
    return pl.pallas_call(
        _inproj_kernel,
        grid=(n // tile,),
        in_specs=[row(d),
                  pl.BlockSpec((None, 1, d), lambda i: (i // tpb, 0, 0)),
                  pl.BlockSpec((None, 1, d), lambda i: (i // tpb, 0, 1)),
                  _const_spec((d, D_IN_PACKED)),
                  row(LANES), row(LANES), row(LANES),
                  _const_spec((1, 256)), _const_spec((1, 256)),
                  _const_spec((GM_GROUPS, GM_CHUNK, GM_CHUNK)), _const_spec((GM_CHUNK, 256))],
        out_specs=[row(w) for w, _ in outs],
        out_shape=[jax.ShapeDtypeStruct((n, w), dt) for w, dt in outs],
        compiler_params=_cparams("parallel"),
        name="inproj",
    )(x2, mod, mod, w_in_p, rc, ra, rb, lng, lnb, gmw, gmb)


def _dsa_kernel(qi_ref, q_ref, aux_ref, kvk_ref, vt_ref, o_ref, key_ref, planes_ref, mask_ref, lg0_ref, lg1_ref,
                *, tq, kc, n_sel):
    j = pl.program_id(1)
    nh = ATT_HEADS
    n_it = ((j + 1) * tq + 2 * kc - 1) // (2 * kc)
    n_rows = n_it * 2 * kc
    nt = (((1,), (1,)), ((), ()))

    @pl.when((pl.program_id(0) == 0) & (j == 0))
    def _():
        planes_ref[...] = jnp.zeros_like(planes_ref)

    qih = jnp.concatenate([qi_ref[:, h * IDX_DIM:(h + 1) * IDX_DIM] for h in range(IDX_HEADS)], axis=0)
    qh = jnp.concatenate([q_ref[:, h * ATT_HEAD_DIM:(h + 1) * ATT_HEAD_DIM] for h in range(nh)], axis=0)
    aux_t = aux_ref[...].T
    w_scale = IDX_HEADS ** -0.5 * IDX_DIM ** -0.5
    wrow = [aux_t[KV_WI - 128 + h:KV_WI - 128 + h + 1, :] * w_scale for h in range(IDX_HEADS)]

    qpos = j * tq + lax.broadcasted_iota(jnp.int32, (kc, tq), 1)
    krel = lax.broadcasted_iota(jnp.int32, (kc, tq), 0)

    last_k0 = kvk_ref.shape[0] - kc

    def logits(col0, q_rows, k0):
        kk = kvk_ref[pl.ds(pl.multiple_of(jnp.minimum(k0, last_k0), kc), kc), col0:col0 + IDX_DIM]
        return lax.dot_general(kk, q_rows, nt, preferred_element_type=F32)

    def pipelined(col0, q_rows, consume, carry, two_ahead):
        lg0_ref[...] = logits(col0, q_rows, 0)
        if two_ahead:
            lg1_ref[...] = logits(col0, q_rows, kc)

        def pair(i, carry):
            k0 = pl.multiple_of(i * 2 * kc, 2 * kc)
            if two_ahead:
                carry = consume(lg0_ref, k0, 2 * i, carry)
                carry = consume(lg1_ref, k0 + kc, 2 * i + 1, carry)
                lg0_ref[...] = logits(col0, q_rows, k0 + 2 * kc)
                lg1_ref[...] = logits(col0, q_rows, k0 + 3 * kc)
                return carry
            lg1_ref[...] = logits(col0, q_rows, k0 + kc)
            carry = consume(lg0_ref, k0, 2 * i, carry)
            lg0_ref[...] = logits(col0, q_rows, k0 + 2 * kc)
            return consume(lg1_ref, k0 + kc, 2 * i + 1, carry)

        return lax.fori_loop(0, n_it, pair, carry)

    def score_chunk(lg_ref, k0, c, carry):
        sc = jnp.zeros((kc, tq), F32)
        for h in range(IDX_HEADS):
            sc = sc + jnp.maximum(lg_ref[:, h * tq:(h + 1) * tq], 0.0) * wrow[h]
        sc = jnp.where(sc == 0.0, 0.0, sc)
        bits = lax.bitcast_convert_type(sc, jnp.int32)
        key = bits ^ ((bits >> 31) & jnp.int32(0x7FFFFFFF))
        key_ref[pl.ds(k0, kc), :] = jnp.where(k0 + krel <= qpos, key, jnp.int32(INT_MIN))
        return carry

    pipelined(KV_KI, qih, score_chunk, 0, two_ahead=True)

    wb = 32
    sbr = wb * SUBLANES

    def to_planes(sb, carry):
        r0 = pl.multiple_of(sb * sbr, sbr)
        rows = pl.ds(pl.multiple_of(sb * SUBLANES, SUBLANES), SUBLANES)
        for l0 in range(0, tq, LANES):
            cols = slice(l0, l0 + LANES)
            a = [key_ref[pl.ds(r0 + SUBLANES * w, SUBLANES), cols] ^ jnp.int32(INT_MIN) for w in range(wb)]
            half, m = wb // 2, 0x0000FFFF
            while half:
                k = 0
                while k < wb:
                    t = (a[k] ^ lax.shift_right_logical(a[k + half], half)) & jnp.int32(np.uint32(m).astype(np.int32))
                    a[k] = a[k] ^ t
                    a[k + half] = a[k + half] ^ (t << half)
                    k = (k + half + 1) & ~half
                half >>= 1
                m ^= (m << half) & 0xFFFFFFFF
            for p in range(wb):
                planes_ref[p, rows, cols] = a[p]
        return carry

    lax.fori_loop(0, n_rows // sbr, to_planes, 0)

    pr = planes_ref.shape[1]
    prow = lax.broadcasted_iota(jnp.int32, (pr, tq), 0)
    live0 = jnp.where(prow < n_rows // wb, jnp.int32(-1), jnp.int32(0))

    def radix_step(live, remaining, plane):
        ones = jnp.sum(lax.population_count(live & plane), axis=0, keepdims=True)
        take = ones >= remaining
        remaining = jnp.where(take, remaining, remaining - ones)
        live = live & (plane ^ jnp.where(take, 0, -1))
        return live, remaining, take

    def score_pass(p, carry):
        live, remaining, thr_u = carry
        live, remaining, take = radix_step(live, remaining, planes_ref[p])
        thr_u = thr_u | jnp.where(take, lax.shift_right_logical(jnp.int32(INT_MIN), p), 0)
        return live, remaining, thr_u

    live, remaining, thr_u = lax.fori_loop(
        0, wb, score_pass, (live0, jnp.full((1, tq), n_sel, jnp.int32), jnp.zeros((1, tq), jnp.int32)))
    thr = thr_u ^ jnp.int32(INT_MIN)

    n_tiles = pr // SUBLANES
    tile_inv = (n_tiles - 1) - (prow >> 3)
    sub_inv = (SUBLANES - 1) - (prow & (SUBLANES - 1))
    lane_bit = [0xAAAAAAAA, 0xCCCCCCCC, 0xF0F0F0F0, 0xFF00FF00, 0xFFFF0000]
    inv_planes = ([(8 + t, -((tile_inv >> t) & 1)) for t in range(max(n_tiles - 1, 1).bit_length())]
                  + [(3 + t, jnp.full((pr, tq), np.uint32(lane_bit[t]).astype(np.int32), jnp.int32)) for t in range(5)]
                  + [(t, -((sub_inv >> t) & 1)) for t in range(3)])
    def tie_cut():
        live_t, rem_t = live, remaining
        inv_cut = jnp.zeros((1, tq), jnp.int32)
        for bit, plane in sorted(inv_planes, key=lambda e: -e[0]):
            live_t, rem_t, take = radix_step(live_t, rem_t, plane)
            inv_cut = inv_cut | jnp.where(take, 1 << bit, 0)
        return (pr * wb - 1) - inv_cut

    n_eq = jnp.sum(lax.population_count(live), axis=0, keepdims=True)
    any_tie = jnp.max(jnp.where(n_eq > remaining, 1, 0)) > 0
    row_cut = lax.cond(any_tie, tie_cut, lambda: jnp.full((1, tq), pr * wb - 1, jnp.int32))

    tb = 256
    rrel = lax.broadcasted_iota(jnp.int32, (tb, tq), 0)

    thr_c = jnp.maximum(thr, jnp.int32(INT_MIN + 1))

    def flag_rows(i, carry):
        r0 = pl.multiple_of(i * tb, tb)
        blk = key_ref[pl.ds(r0, tb), :]
        keep = (blk > thr_c) | ((blk == thr_c) & (r0 + rrel <= row_cut))
        mask_ref[pl.ds(r0, tb), :] = jnp.where(keep, 0.0, NEG_BIG)
        return carry

    lax.fori_loop(0, n_rows // tb, flag_rows, 0)

    def att_chunk(lg_ref, k0, c, carry):
        m, acc = carry
        madd = mask_ref[pl.ds(k0, kc), :]
        vt_c = vt_ref[c]
        m_parts, acc_parts = [], []
        for h in range(nh):
            hs = slice(h * tq, (h + 1) * tq)
            lgh = lg_ref[:, hs] + madd
            mh = jnp.maximum(m[:, hs], jnp.max(lgh, axis=0, keepdims=True))
            m_parts.append(mh)
            ph = jnp.exp2(lgh - mh).astype(BF16)
            acc_parts.append(acc[:, hs] * jnp.exp2(m[:, hs] - mh) + jnp.dot(vt_c, ph, preferred_element_type=F32))
        return jnp.concatenate(m_parts, axis=1), jnp.concatenate(acc_parts, axis=1)

    m0 = jnp.full((1, nh * tq), NEG_BIG, F32)
    a0 = jnp.zeros((vt_ref.shape[1], nh * tq), F32)
    _, acc = pipelined(KV_K, qh, att_chunk, (m0, a0), two_ahead=False)
    o = acc[0:ATT_HEAD_DIM] / acc[ATT_HEAD_DIM:ATT_HEAD_DIM + 1]
    o = jnp.concatenate([o, jnp.zeros_like(o)], axis=0)
    for h in range(nh):
        o_ref[:, h * ATT_HEAD_DIM:(h + 1) * ATT_HEAD_DIM] = (
            o[:, h * tq:(h + 1) * tq].T[:, :ATT_HEAD_DIM].astype(o_ref.dtype))


DSA_KEY_CHUNK = 256


DSA_QUERY_TILE = 256
DSA_VT_ROWS = 80


def dsa(qi, q, aux, kvk, vt, batch, seq, tq=DSA_QUERY_TILE, kc=DSA_KEY_CHUNK):
    n = qi.shape[0]
    nq = seq // tq
    n_sel = min(TOPK_MAX, seq // 4)
    blk = lambda w: pl.BlockSpec((tq, w), lambda b, j: (b * nq + j, 0))
    return pl.pallas_call(
        functools.partial(_dsa_kernel, tq=tq, kc=kc, n_sel=n_sel),
        grid=(batch, nq),
        in_specs=[blk(256), blk(256), blk(LANES),
                  pl.BlockSpec((seq, 256), lambda b, j: (b, 0), pipeline_mode=pl.Buffered(1)),
                  pl.BlockSpec((None, seq // kc, vt.shape[2], kc), lambda b, j: (b, 0, 0, 0),
                               pipeline_mode=pl.Buffered(1))],
        out_specs=blk(256),
        out_shape=jax.ShapeDtypeStruct((n, 256), BF16),
        scratch_shapes=[pltpu.VMEM((seq, tq), jnp.int32),
                        pltpu.VMEM((32, seq // 32, tq), jnp.int32),
                        pltpu.VMEM((seq, tq), F32),
                        pltpu.VMEM((kc, ATT_HEADS * tq), F32),
                        pltpu.VMEM((kc, ATT_HEADS * tq), F32)],
        compiler_params=_cparams("arbitrary", "arbitrary"),
        name="dsa",
    )(qi, q, aux, kvk, vt)


def s5_tables(lam_re, lam_im, log_step, b_re, b_im, c_re, c_im):
    hp = lax.Precision.HIGHEST
    L = S5_CHUNK
    g_, p_ = lam_re.shape
    step = jnp.exp(log_step)[:, None]
    d = jnp.arange(L + 1, dtype=F32)[None, :, None]
    mag = jnp.exp(d * (lam_re * step)[:, None, :])
    ang = d * (lam_im * step)[:, None, :]
    pw_re, pw_im = mag * jnp.cos(ang), mag * jnp.sin(ang)
    ab_re, ab_im = pw_re[:, 1], pw_im[:, 1]
    den = lam_re * lam_re + lam_im * lam_im
    nr = ab_re - 1.0
    cr = (nr * lam_re + ab_im * lam_im) / den
    ci = (ab_im * lam_re - nr * lam_im) / den
    bb_re = cr[..., None] * b_re - ci[..., None] * b_im
    bb_im = cr[..., None] * b_im + ci[..., None] * b_re
    cp_re = c_re[:, None] * pw_re[:, :, None, :] - c_im[:, None] * pw_im[:, :, None, :]
    cp_im = c_re[:, None] * pw_im[:, :, None, :] + c_im[:, None] * pw_re[:, :, None, :]
    kern = (jnp.einsum('gdop,gpi->gdoi', cp_re, bb_re, precision=hp)
            - jnp.einsum('gdop,gpi->gdoi', cp_im, bb_im, precision=hp))
    gp = S5_PACK
    npk = g_ // gp
    pk = lambda a: a.reshape((npk, gp) + a.shape[1:])

    kt = pk(kern[:, :L].transpose(0, 1, 3, 2))
    bd = jnp.zeros((npk, L, gp, S5_GROUP, gp, S5_GROUP), F32)
    for g in range(gp):
        bd = bd.at[:, :, g, :, g, :].set(kt[:, g])
    bd = bd.reshape(npk, L, gp * S5_GROUP, gp * S5_GROUP).astype(BF16)
    d_rev = jnp.arange(L - 1, -1, -1).astype(F32)[None, :, None]
    mag_rev = jnp.exp(d_rev * (lam_re * step)[:, None, :])
    ang_rev = d_rev * (lam_im * step)[:, None, :]
    pw_rev_re, pw_rev_im = mag_rev * jnp.cos(ang_rev), mag_rev * jnp.sin(ang_rev)
    bt_re, bt_im = bb_re.transpose(0, 2, 1)[:, None], bb_im.transpose(0, 2, 1)[:, None]
    gm_re = pw_rev_re[:, :, None, :] * bt_re - pw_rev_im[:, :, None, :] * bt_im
    gm_im = pw_rev_re[:, :, None, :] * bt_im + pw_rev_im[:, :, None, :] * bt_re
    gc = jnp.concatenate([pk(gm_re), pk(gm_im)], axis=-1).transpose(0, 2, 1, 3, 4)
    gc = gc.reshape(npk, L * gp * S5_GROUP, 2 * p_).astype(BF16)
    e_re = cp_re[:, 1:].transpose(0, 3, 1, 2).reshape(g_, p_, L * S5_GROUP)
    e_im = -cp_im[:, 1:].transpose(0, 3, 1, 2).reshape(g_, p_, L * S5_GROUP)
    ec = pk(jnp.concatenate([e_re, e_im], axis=1)).astype(BF16)
    al = jnp.stack([pw_re[:, L].reshape(npk, gp * p_), pw_im[:, L].reshape(npk, gp * p_)], axis=1)
    return bd, gc, ec, al


def _s5_kernel(u_ref, bd_ref, gc_ref, ec_ref, al_ref, y_ref, w_ref, e_ref, ys_ref, hp_ref, state_ref,
               *, n_chunks, nb):
    lw = u_ref.shape[1]
    p_ = al_ref.shape[1]
    n_lag = bd_ref.shape[0]
    gp, ps, cw = S5_PACK, S5_STATE, S5_GROUP

    @pl.when(pl.program_id(1) == 0)
    def _():
        state_ref[...] = jnp.zeros_like(state_ref)
        w_ref[...] = jnp.zeros_like(w_ref)
        e_ref[...] = jnp.zeros_like(e_ref)
        for s in range(n_lag):
            for t in range(s, n_lag):
                w_ref[s * LANES:(s + 1) * LANES, t * LANES:(t + 1) * LANES] = bd_ref[t - s]
            for g in range(gp):
                rows = slice(s * LANES + g * cw, s * LANES + (g + 1) * cw)
                w_ref[rows, lw + g * ps:lw + (g + 1) * ps] = gc_ref[rows, 0:ps]
                w_ref[rows, lw + p_ + g * ps:lw + p_ + (g + 1) * ps] = gc_ref[rows, ps:2 * ps]
        for h in range(gp):
            for t in range(n_lag):
                cols = slice(t * LANES + h * cw, t * LANES + (h + 1) * cw)
                e_ref[h * ps:(h + 1) * ps, cols] = ec_ref[h, 0:ps, t * cw:(t + 1) * cw]
                e_ref[p_ + h * ps:p_ + (h + 1) * ps, cols] = ec_ref[h, ps:2 * ps, t * cw:(t + 1) * cw]

    ys_ref[...] = jnp.dot(u_ref[...], w_ref[...], preferred_element_type=F32)
    ar = al_ref[0:1, :]
    ai = al_ref[1:2, :]

    cpi = S5_CHUNKS_PER_ITER
    rows_per_iter = cpi * nb
    assert rows_per_iter % SUBLANES == 0 and n_chunks % cpi == 0

    def step(it, carry):
        hr, hi = carry
        rows = pl.ds(pl.multiple_of(it * rows_per_iter, rows_per_iter), rows_per_iter)
        st = ys_ref[rows, lw:lw + 2 * p_]
        hrs, his = [], []
        for k in range(cpi):
            hrs.append(hr)
            his.append(hi)
            sr, si = st[k * nb:(k + 1) * nb, 0:p_], st[k * nb:(k + 1) * nb, p_:2 * p_]
            hr, hi = ar * hr - ai * hi + sr, ar * hi + ai * hr + si
        hp_ref[rows, 0:p_] = jnp.concatenate(hrs, axis=0)
        hp_ref[rows, p_:2 * p_] = jnp.concatenate(his, axis=0)
        return hr, hi

    hr, hi = lax.fori_loop(0, n_chunks // cpi, step, (state_ref[0], state_ref[1]))
    state_ref[0] = hr
    state_ref[1] = hi
    y_ref[...] = (ys_ref[:, 0:lw] + jnp.dot(hp_ref[...].astype(BF16), e_ref[...], preferred_element_type=F32)
                  ).astype(y_ref.dtype)


def s5_scan(u_p, bd, gc, ec, al, n_chunks, nb):
    npk, r, lw = u_p.shape
    sw = al.shape[2]
    n_tiles = max(1, r // S5_ROW_TILE)
    rt = r // n_tiles
    per_pack = lambda a: pl.BlockSpec((None,) + a.shape[1:], lambda g, i: (g,) + (0,) * (a.ndim - 1))
    return pl.pallas_call(
        functools.partial(_s5_kernel, n_chunks=n_chunks // n_tiles, nb=nb),
        grid=(npk, n_tiles),
        in_specs=[pl.BlockSpec((None, rt, lw), lambda g, i: (g, i, 0)),
                  per_pack(bd), per_pack(gc), per_pack(ec), per_pack(al)],
        out_specs=pl.BlockSpec((None, rt, lw), lambda g, i: (g, i, 0)),
        out_shape=jax.ShapeDtypeStruct((npk, r, lw), BF16),
        scratch_shapes=[pltpu.VMEM((lw, lw + 2 * sw), BF16), pltpu.VMEM((2 * sw, lw), BF16),
                        pltpu.VMEM((rt, lw + 2 * sw), F32), pltpu.VMEM((rt, 2 * sw), F32),
                        pltpu.VMEM((2, nb, sw), F32)],
        compiler_params=_cparams("arbitrary", "arbitrary"),
        name="s5",
    )(u_p, bd, gc, ec, al)


def _ssd_kernel(z_ref, xbc_ref, aux_ref, cw_ref, cb_ref, dtb_ref, aneg_ref, dsk_ref, ng_ref,
                y_ref, state_ref, halo_ref, *, tile):
    hp = lax.Precision.HIGHEST
    ck = SSD_CHUNK
    hd = SSD_HEAD_DIM

    @pl.when(pl.program_id(1) == 0)
    def _():
        state_ref[...] = jnp.zeros_like(state_ref)
        halo_ref[...] = jnp.zeros_like(halo_ref)

    xbc = xbc_ref[...]
    ext = jnp.concatenate([halo_ref[...], xbc], axis=0)
    halo_ref[...] = xbc[tile - SUBLANES:tile, :]
    conv = cb_ref[...] + sum(cw_ref[jj:jj + 1, :] * ext[SUBLANES - (SSD_CONV - 1) + jj:SUBLANES - (SSD_CONV - 1) + jj + tile, :]
                             for jj in range(SSD_CONV))
    act = _silu(conv)
    x = act[:, 0:BRANCH_W]
    bm = act[:, BRANCH_W:BRANCH_W + SSD_GROUPS * SSD_STATE]
    cm = act[:, BRANCH_W + SSD_GROUPS * SSD_STATE:]
    dt_all = jax.nn.softplus(aux_ref[...] + dtb_ref[...])
    adt_all = dt_all * aneg_ref[...]

    r_i = lax.broadcasted_iota(jnp.int32, (ck, ck), 0)
    c_i = lax.broadcasted_iota(jnp.int32, (ck, ck), 1)
    lower = c_i <= r_i
    tri = lower.astype(F32)
    zs = _silu(z_ref[...])
    ng = ng_ref[...]
    dl = KV_DT - 128

    for ci in range(tile // ck):
        rows = slice(ci * ck, (ci + 1) * ck)
        a_cum = jnp.dot(tri, adt_all[rows], preferred_element_type=F32, precision=hp)
        a_cum_t = a_cum.T
        bm_t = bm[rows].T
        cmb = cm[rows].astype(BF16)
        bmb = bm[rows].astype(BF16)
        gmat = [lax.dot_general(cmb[:, g * SSD_STATE:(g + 1) * SSD_STATE], bmb[:, g * SSD_STATE:(g + 1) * SSD_STATE],
                                (((1,), (1,)), ((), ())), preferred_element_type=F32) for g in range(SSD_GROUPS)]
        ys = []
        for h in range(SSD_HEADS):
            g = h // (SSD_HEADS // SSD_GROUPS)
            acol = a_cum[:, dl + h:dl + h + 1]
            arow = a_cum_t[dl + h:dl + h + 1, :]
            dtc = dt_all[rows, dl + h:dl + h + 1]
            xh = x[rows, h * hd:(h + 1) * hd]
            xdt = xh * dtc
            lmat = jnp.exp(jnp.where(lower, acol - arow, NEG_BIG))
            y_diag = jnp.dot((gmat[g] * lmat).astype(BF16), xdt.astype(BF16), preferred_element_type=F32)
            s_prev = state_ref[h]
            y_off = jnp.exp(acol) * jnp.dot(cmb[:, g * SSD_STATE:(g + 1) * SSD_STATE], s_prev.astype(BF16),
                                            preferred_element_type=F32)
            a_last = a_cum[ck - 1:ck, dl + h:dl + h + 1]
            decay = jnp.exp(a_last - acol)
            s_new = jnp.dot(bm_t[g * SSD_STATE:(g + 1) * SSD_STATE, :].astype(BF16), (xdt * decay).astype(BF16),
                            preferred_element_type=F32)
            state_ref[h] = jnp.exp(a_last) * s_prev + s_new
            ys.append(y_diag + y_off + dsk_ref[:, h * hd:(h + 1) * hd] * xh)
        y = jnp.concatenate(ys, axis=1) * zs[rows]
        gw = BRANCH_W // SSD_GROUPS
        outs = []
        for g in range(SSD_GROUPS):
            yg = y[:, g * gw:(g + 1) * gw]
            outs.append(yg * lax.rsqrt(jnp.mean(yg * yg, axis=-1, keepdims=True) + RMS_EPS))
        y_ref[rows, :] = (jnp.concatenate(outs, axis=1) * ng).astype(y_ref.dtype)


def ssd(z, xbc, aux, cw, cb, dtb, aneg, dsk, ng, batch, seq, tile=512):
    n = z.shape[0]
    nt = seq // tile
    row = lambda w: pl.BlockSpec((tile, w), lambda b, i: (b * nt + i, 0))
    return pl.pallas_call(
        functools.partial(_ssd_kernel, tile=tile),
        grid=(batch, nt),
        in_specs=[row(256), row(512), row(LANES),
                  _const_spec((SSD_CONV, SSD_XBC)), _const_spec((1, SSD_XBC)), _const_spec((1, LANES)),
                  _const_spec((1, LANES)), _const_spec((1, 256)), _const_spec((1, 256))],
        out_specs=row(256),
        out_shape=jax.ShapeDtypeStruct((n, 256), BF16),
        scratch_shapes=[pltpu.VMEM((SSD_HEADS, SSD_STATE, SSD_HEAD_DIM), F32), pltpu.VMEM((SUBLANES, SSD_XBC), F32)],
        compiler_params=_cparams("parallel", "arbitrary"),
        name="ssd",
    )(z, xbc, aux, cw, cb, dtb, aneg, dsk, ng)


def _merge_kernel(x_ref, shift_ref, scale_ref, gate_ref, ya_ref, yb_ref, yc_ref, s5u_ref, yd_ref,
                  s5d_ref, gluw_ref, glub_ref, wg_ref, wb_ref, wo_ref, lng_ref, lnb_ref, o_ref):
    x = x_ref[...]
    h = (x * (1.0 + scale_ref[...]) + shift_ref[...]).astype(BF16)
    yc = jax.nn.gelu(yc_ref[...] + s5d_ref[...] * s5u_ref[...])
    yc = yc * _sigmoid(jnp.dot(yc.astype(BF16), gluw_ref[...], preferred_element_type=F32) + glub_ref[...])
    branches = (ya_ref[...], yb_ref[...], yc.astype(BF16), yd_ref[...])
    merged = None
    for i, br in enumerate(branches):
        gate = _sigmoid(jnp.dot(h, wg_ref[i], preferred_element_type=F32))
        term = gate * jnp.dot(br, wb_ref[i], preferred_element_type=F32)
        merged = term if merged is None else merged + term
    mix = jnp.dot(merged.astype(BF16), wo_ref[...], preferred_element_type=F32)
    o_ref[...] = _layer_norm(DN_ALPHA * x + (1.0 + gate_ref[...]) * mix, lng_ref[...], lnb_ref[...])


def merge(x2, mod, ya, yb, yc, s5u, yd, s5d, gluw, glub, wg, wb, wo, lng, lnb, seq, tile=512):
    n, d = x2.shape
    tpb = seq // tile
    row = lambda w: pl.BlockSpec((tile, w), lambda i: (i, 0))
    modspec = lambda k: pl.BlockSpec((None, 1, d), lambda i: (i // tpb, 0, k))
    return pl.pallas_call(
        _merge_kernel,
        grid=(n // tile,),
        in_specs=[row(d), modspec(0), modspec(1), modspec(2),
                  row(256), row(256), row(256), row(256), row(256),
                  _const_spec((1, 256)), _const_spec((256, 256)), _const_spec((1, 256)),
                  _const_spec((4, d, d)), _const_spec((4, 256, d)), _const_spec((d, d)),
                  _const_spec((1, d)), _const_spec((1, d))],
        out_specs=row(d),
        out_shape=jax.ShapeDtypeStruct((n, d), F32),
        compiler_params=_cparams("parallel"),
        name="merge",
    )(x2, mod, mod, mod, ya, yb, yc, s5u, yd, s5d, gluw, glub, wg, wb, wo, lng, lnb)


def _route(scores_t, bias_ref):
    ng = N_EXPERT_GROUPS
    per = N_EXPERTS // ng
    t = scores_t.shape[1]
    a = [scores_t[k * ng:(k + 1) * ng] for k in range(per)]
    b = [a[k] + bias_ref[k * ng:(k + 1) * ng, :] for k in range(per)]
    hi1, lo1 = jnp.maximum(b[0], b[1]), jnp.minimum(b[0], b[1])
    hi2, lo2 = jnp.maximum(b[2], b[3]), jnp.minimum(b[2], b[3])
    gs = jnp.maximum(hi1, hi2) + jnp.maximum(jnp.minimum(hi1, hi2), jnp.maximum(lo1, lo2))
    gidx = lax.broadcasted_iota(jnp.int32, (ng, t), 0)
    rank = jnp.zeros((ng, t), jnp.int32)
    for g2 in range(ng):
        row = gs[g2:g2 + 1, :]
        rank = rank + ((row > gs) | ((row == gs) & (g2 < gidx))).astype(jnp.int32)
    gmask = rank < TOPK_GROUPS
    neg_inf = jnp.float32(-jnp.inf)
    masked = [jnp.where(gmask, b[k], neg_inf) for k in range(per)]
    eidx = [gidx * per + k for k in range(per)]
    sel = [jnp.zeros((ng, t), jnp.bool_) for _ in range(per)]
    for _ in range(TOP_K):
        m = jnp.max(jnp.maximum(jnp.maximum(masked[0], masked[1]), jnp.maximum(masked[2], masked[3])),
                    axis=0, keepdims=True)
        cand = [jnp.where(masked[k] == m, eidx[k], N_EXPERTS) for k in range(per)]
        imin = jnp.min(jnp.minimum(jnp.minimum(cand[0], cand[1]), jnp.minimum(cand[2], cand[3])),
                       axis=0, keepdims=True)
        for k in range(per):
            hit = eidx[k] == imin
            sel[k] = sel[k] | hit
            masked[k] = jnp.where(hit, neg_inf, masked[k])
    w = [jnp.where(sel[k], a[k], 0.0) for k in range(per)]
    denom = jnp.sum(w[0] + w[1] + w[2] + w[3], axis=0, keepdims=True)
    gates = [w[k] / denom * ROUTED_SCALE for k in range(per)]
    return jnp.concatenate(gates + [jnp.zeros((LANES - N_EXPERTS, t), F32)], axis=0)


MOE_EXPERTS_PER_STEP = 4


def _moe_kernel(x_ref, shift_ref, scale_ref, gate_ref, rw_ref, rb_ref, *refs):
    eps = MOE_EXPERTS_PER_STEP
    w_refs = refs[:2 * eps]
    sw13_ref, sw2_ref, lng_ref, lnb_ref, o_ref, h_ref, acc_ref, g_ref = refs[2 * eps:]
    j = pl.program_id(1)
    ff = EXPERT_FF

    @pl.when(j == 0)
    def _():
        h = (x_ref[...] * (1.0 + scale_ref[...]) + shift_ref[...]).astype(BF16)
        h_ref[...] = h
        scores = _sigmoid(jnp.dot(h, rw_ref[...], preferred_element_type=F32))
        g_ref[...] = _route(scores.T, rb_ref).T
        ab = jnp.dot(h, sw13_ref[...], preferred_element_type=F32)
        act = _silu(ab[:, :ff]) * ab[:, ff:]
        acc_ref[...] = jnp.dot(act.astype(BF16), sw2_ref[...], preferred_element_type=F32)

    h = h_ref[...]
    lane = lax.broadcasted_iota(jnp.int32, g_ref.shape, 1)
    gates = g_ref[...]
    out = None
    for k in range(eps):
        w13_ref, w2_ref = w_refs[2 * k], w_refs[2 * k + 1]
        ab = jnp.dot(h, w13_ref[...], preferred_element_type=F32)
        gcol = jnp.sum(jnp.where(lane == eps * j + k, gates, 0.0), axis=1, keepdims=True)
        act = _silu(ab[:, :ff]) * ab[:, ff:] * gcol
        y = jnp.dot(act.astype(BF16), w2_ref[...], preferred_element_type=F32)
        out = y if out is None else out + y
    acc_ref[...] += out

    @pl.when(j == pl.num_programs(1) - 1)
    def _():
        o_ref[...] = _layer_norm(DN_ALPHA * x_ref[...] + (1.0 + gate_ref[...]) * acc_ref[...],
                                 lng_ref[...], lnb_ref[...])


def moe(x2, mod, rw, rb, w13, w2, sw13, sw2, lng, lnb, seq, tile=1024):
    n, d = x2.shape
    tpb = seq // tile
    ne = w13.shape[0]
    per = N_EXPERTS // N_EXPERT_GROUPS
    emap = lambda j: (j % N_EXPERT_GROUPS) * per + j // N_EXPERT_GROUPS
    row = pl.BlockSpec((tile, d), lambda i, j: (i, 0))
    modspec = lambda k: pl.BlockSpec((None, 1, d), lambda i, j: (i // tpb, 0, k))
    eps = MOE_EXPERTS_PER_STEP
    w13spec = lambda k: pl.BlockSpec((None, d, 2 * EXPERT_FF), lambda i, j: (emap(eps * j + k), 0, 0))
    w2spec = lambda k: pl.BlockSpec((None, EXPERT_FF, d), lambda i, j: (emap(eps * j + k), 0, 0))
    w_specs = [spec(k) for k in range(eps) for spec in (w13spec, w2spec)]
    return pl.pallas_call(
        _moe_kernel,
        grid=(n // tile, ne // eps),
        in_specs=[row, modspec(0), modspec(1), modspec(2),
                  _const_spec((d, LANES)), _const_spec((N_EXPERTS, 1)),
                  *w_specs,
                  _const_spec((d, 2 * EXPERT_FF)), _const_spec((EXPERT_FF, d)),
                  _const_spec((1, d)), _const_spec((1, d))],
        out_specs=row,
        out_shape=jax.ShapeDtypeStruct((n, d), F32),
        scratch_shapes=[pltpu.VMEM((tile, d), BF16), pltpu.VMEM((tile, d), F32), pltpu.VMEM((tile, LANES), F32)],
        compiler_params=_cparams("parallel", "arbitrary"),
        name="moe",
    )(x2, mod, mod, mod, rw, rb, *([w13, w2] * eps), sw13, sw2, lng, lnb)


def _pack_w_in(w):
    d = w.shape[0]
    cuts = np.cumsum([0, 256, 256, 256, 64, 64, 256, 64, 4, 256, 256, 512, 4])
    gm_u, gm_v, q, k, v, qi, ki, wi, s5, z, xbc, dt = [w[:, cuts[i]:cuts[i + 1]] for i in range(12)]
    pad = jnp.zeros((d, 256 - (64 * 3 + 8)), w.dtype)
    return jnp.concatenate([gm_u, gm_v, q, qi, k, v, ki, wi, dt, pad, s5, z, xbc], axis=1).astype(BF16)


def _rope_tables(positions):
    rot = 2 * ROPE_HALF
    inv = ROPE_THETA ** (-jnp.arange(0, rot, 2, dtype=F32) / rot)
    ang = positions.astype(F32).reshape(-1, 1) * inv
    cos, sin = jnp.cos(ang), jnp.sin(ang)
    n = ang.shape[0]
    rest = ATT_HEAD_DIM - rot
    one = jnp.ones((n, rest), F32)
    zero = jnp.zeros((n, rest), F32)
    z8 = jnp.zeros((n, ROPE_HALF), F32)
    rc = jnp.tile(jnp.concatenate([cos, cos, one], axis=1), (1, 2))
    ra = jnp.tile(jnp.concatenate([-sin, z8, zero], axis=1), (1, 2))
    rb = jnp.tile(jnp.concatenate([z8, sin, zero], axis=1), (1, 2))
    return rc, ra, rb


def _values_t(kvk, bsz, seq, kc):
    v = kvk[:, KV_V:KV_V + ATT_HEAD_DIM].reshape(bsz, seq // kc, kc, ATT_HEAD_DIM).transpose(0, 1, 3, 2)
    ones = jnp.ones((bsz, seq // kc, 1, kc), v.dtype)
    pad = jnp.zeros((bsz, seq // kc, DSA_VT_ROWS - ATT_HEAD_DIM - 1, kc), v.dtype)
    return jnp.concatenate([v, ones, pad], axis=2)


def _lane_pad(vec, offset):
    out = jnp.zeros((1, LANES), F32)
    return out.at[0, offset:offset + vec.shape[0]].set(vec.astype(F32))


def kernel(x, c, positions, mod1_w, mod1_b, w_in, gm_ln_g, gm_ln_b, gm_w, gm_b, s5_lam_re, s5_lam_im, s5_log_step, s5_b_re, s5_b_im, s5_c_re, s5_c_im, s5_d, s5_glu_w, s5_glu_b, ssd_conv_w, ssd_conv_b, ssd_dt_bias, ssd_a_log, ssd_d, ssd_norm_g, w_branch, w_gate, w_out, ln1_g, ln1_b, mod2_w, mod2_b, router_w, router_bias, exp_w1, exp_w3, exp_w2, sh_w1, sh_w3, sh_w2, ln2_g, ln2_b):
    bsz, seq, d = x.shape
    n = bsz * seq
    depth = w_in.shape[0]
    x2 = x.reshape(n, d)

    c_pad = jnp.zeros((SUBLANES, d), F32).at[:bsz].set(c)
    mods1 = adaln_all(c_pad, mod1_w, mod1_b[:, None, :])[:, :, None, :]
    mods2 = adaln_all(c_pad, mod2_w, mod2_b[:, None, :])[:, :, None, :]
    rc, ra, rb = _rope_tables(positions)

    tri = jnp.tril(jnp.ones((GM_CHUNK, GM_CHUNK), F32))
    per = N_EXPERTS // N_EXPERT_GROUPS
    eperm = np.array([(r % N_EXPERT_GROUPS) * per + r // N_EXPERT_GROUPS for r in range(N_EXPERTS)])
    kc = DSA_KEY_CHUNK
    L = S5_CHUNK
    n_chunks = seq // L

    for l in range(depth):
        ya, q, qi, kvk, aux, s5u, z, xbc = inproj(
            x2, mods1[l], _pack_w_in(w_in[l]), rc, ra, rb,
            gm_ln_g[l][None], gm_ln_b[l][None], (gm_w[l] * tri).astype(BF16),
            jnp.repeat(gm_b[l].T, BRANCH_W // GM_GROUPS, axis=1), seq, 512)

        yb = dsa(qi, q, aux, kvk, _values_t(kvk, bsz, seq, kc), bsz, seq, kc=kc)

        s5_tabs = s5_tables(s5_lam_re[l], s5_lam_im[l], s5_log_step[l], s5_b_re[l], s5_b_im[l],
                            s5_c_re[l], s5_c_im[l])
        npk = S5_GROUPS // S5_PACK
        u_p = (s5u.astype(BF16).reshape(bsz, n_chunks, L, npk, LANES)
               .transpose(3, 1, 0, 2, 4).reshape(npk, n_chunks * bsz, L * LANES))
        y_p = s5_scan(u_p, *s5_tabs, n_chunks, bsz)
        yc = y_p.reshape(npk, n_chunks, bsz, L, LANES).transpose(2, 1, 3, 0, 4).reshape(n, BRANCH_W)

        yd = ssd(z, xbc, aux, ssd_conv_w[l], ssd_conv_b[l][None],
                 _lane_pad(ssd_dt_bias[l], KV_DT - 128), _lane_pad(-jnp.exp(ssd_a_log[l]), KV_DT - 128),
                 jnp.repeat(ssd_d[l], SSD_HEAD_DIM)[None], ssd_norm_g[l][None], bsz, seq)

        x2 = merge(x2, mods1[l], ya, yb, yc, s5u, yd, s5_d[l][None], s5_glu_w[l].astype(BF16), s5_glu_b[l][None],
                   w_gate[l].astype(BF16), w_branch[l].astype(BF16), w_out[l].astype(BF16),
                   ln1_g[l][None], ln1_b[l][None], seq)

        rw = jnp.zeros((d, LANES), F32).at[:, :N_EXPERTS].set(router_w[l][:, eperm]).astype(BF16)
        x2 = moe(x2, mods2[l], rw, router_bias[l][eperm][:, None],
                 jnp.concatenate([exp_w1[l], exp_w3[l]], axis=2).astype(BF16), exp_w2[l].astype(BF16),
                 jnp.concatenate([sh_w1[l], sh_w3[l]], axis=1).astype(BF16), sh_w2[l].astype(BF16),
                 ln2_g[l][None], ln2_b[l][None], seq)

    return x2.reshape(bsz, seq, d)
```
